```python
import math
import jax
import jax.numpy as jnp
from jax import lax
import numpy as np

D_MODEL = 1024
BATCH = 8
SEQ = 2048
DEPTH = 4

N_EVEN = (DEPTH + 1) // 2
N_ODD = DEPTH // 2

NORM_EPS = 1e-6

RG_W = D_MODEL
RG_BLOCKS = 8
RG_BLOCK_W = RG_W // RG_BLOCKS
RG_C = 8.0
CONV_WIDTH = 4

MLA_HEADS = 8
MLA_NOPE = 128
MLA_ROPE = 64
MLA_V = 128
MLA_QK = MLA_NOPE + MLA_ROPE
Q_LORA = 512
KV_LORA = 256
ROPE_THETA = 10000.0
Q_BLOCK = 128

HY_SIZES = (RG_W, RG_W, Q_LORA, KV_LORA, MLA_ROPE)
HY_IN = RG_W + RG_W + Q_LORA + KV_LORA + MLA_ROPE
HY_SPLITS = [RG_W, 2 * RG_W, 2 * RG_W + Q_LORA, 2 * RG_W + Q_LORA + KV_LORA]
HY_MIX = RG_W + MLA_HEADS * MLA_V

GDN_K_HEADS = 8
GDN_V_HEADS = 16
GDN_HEAD_DIM = 128
GDN_QK_W = GDN_K_HEADS * GDN_HEAD_DIM
GDN_V_W = GDN_V_HEADS * GDN_HEAD_DIM
GDN_CONV_W = 2 * GDN_QK_W + GDN_V_W
GDN_IN = GDN_CONV_W + GDN_V_W + 2 * GDN_V_HEADS
GDN_SPLITS = [GDN_CONV_W, GDN_CONV_W + GDN_V_W, GDN_CONV_W + GDN_V_W + GDN_V_HEADS]
GDN_CHUNK = 64

FFN_HIDDEN = -(-8 * D_MODEL // (3 * 256)) * 256

kernel_name = "hybrid_rglru_mla_gdn_sandwich"


def rms_norm(x, w):
    xf = x.astype(jnp.float32)
    y = xf * lax.rsqrt(jnp.mean(xf * xf, axis=-1, keepdims=True) + NORM_EPS)
    return (y * w.astype(jnp.float32)).astype(x.dtype)


def l2_norm(x):
    xf = x.astype(jnp.float32)
    return xf * lax.rsqrt(jnp.sum(xf * xf, axis=-1, keepdims=True) + NORM_EPS)


def causal_depthwise_conv(x, w):
    width, chans = w.shape
    return lax.conv_general_dilated(
        x, w[:, None, :].astype(x.dtype), window_strides=(1,),
        padding=((width - 1, 0),), dimension_numbers=("NWC", "WIO", "NWC"),
        feature_group_count=chans)


def rope_tables(positions):
    inv_freq = 1.0 / (ROPE_THETA ** (jnp.arange(0, MLA_ROPE, 2, dtype=jnp.float32) / MLA_ROPE))
    ang = positions.astype(jnp.float32)[..., None] * inv_freq
    return jnp.cos(ang), jnp.sin(ang)


def apply_rope(x, cos, sin):
    half = x.shape[-1] // 2
    c = cos[:, :, None, :].astype(x.dtype)
    s = sin[:, :, None, :].astype(x.dtype)
    x1, x2 = x[..., :half], x[..., half:]
    return jnp.concatenate([x1 * c - x2 * s, x2 * c + x1 * s], axis=-1)


def causal_block_attention(q, k, v, scale):
    T = q.shape[1]
    outs = []
    for start in range(0, T, Q_BLOCK):
        end = min(start + Q_BLOCK, T)
        qb, kb, vb = q[:, start:end], k[:, :end], v[:, :end]
        s = jnp.einsum('bqhd,bkhd->bhqk', qb, kb).astype(jnp.float32) * scale
        mask = (start + jnp.arange(end - start))[:, None] >= jnp.arange(end)[None, :]
        p = jax.nn.softmax(jnp.where(mask, s, -jnp.inf), axis=-1).astype(v.dtype)
        outs.append(jnp.einsum('bhqk,bkhd->bqhd', p, vb))
    return jnp.concatenate(outs, axis=1)


def _lru_combine(e1, e2):
    a1, b1 = e1
    a2, b2 = e2
    return a1 * a2, a2 * b1 + b2


def rglru_branch(x_r, gate_r, conv_w, conv_b, gate_a_w, gate_a_b, gate_x_w, gate_x_b, lam):
    B, T, _ = x_r.shape
    xc = causal_depthwise_conv(x_r, conv_w) + conv_b.astype(x_r.dtype)
    xb = xc.reshape(B, T, RG_BLOCKS, RG_BLOCK_W)
    r = jax.nn.sigmoid((jnp.einsum('btni,nij->btnj', xb, gate_a_w).reshape(B, T, RG_W)
                        + gate_a_b).astype(jnp.float32))
    i = jax.nn.sigmoid((jnp.einsum('btni,nij->btnj', xb, gate_x_w).reshape(B, T, RG_W)
                        + gate_x_b).astype(jnp.float32))
    log_a = -RG_C * r * jax.nn.softplus(-lam.astype(jnp.float32))
    a = jnp.exp(log_a)
    b = jnp.sqrt(-jnp.expm1(2.0 * log_a)) * (i * xc.astype(jnp.float32))
    _, h = lax.associative_scan(_lru_combine, (a, b), axis=1)
    return h.astype(x_r.dtype) * jax.nn.gelu(gate_r)


def mla_branch(c_q, c_kv, k_pe, cos, sin, q_norm, w_uq, kv_norm, w_ukv):
    B, T, _ = c_q.shape
    q = (rms_norm(c_q, q_norm) @ w_uq).reshape(B, T, MLA_HEADS, MLA_QK)
    q = jnp.concatenate([q[..., :MLA_NOPE], apply_rope(q[..., MLA_NOPE:], cos, sin)], axis=-1)
    kv = (rms_norm(c_kv, kv_norm) @ w_ukv).reshape(B, T, MLA_HEADS, MLA_NOPE + MLA_V)
    k_nope, v = kv[..., :MLA_NOPE], kv[..., MLA_NOPE:]
    k_rope = apply_rope(k_pe[:, :, None, :], cos, sin)
    k = jnp.concatenate([k_nope, jnp.broadcast_to(k_rope, (B, T, MLA_HEADS, MLA_ROPE))], axis=-1)
    o = causal_block_attention(q, k, v, MLA_QK ** -0.5)
    return o.reshape(B, T, MLA_HEADS * MLA_V)


def hybrid_rglru_mla(h, cos, sin, w_in, conv_w, conv_b, gate_a_w, gate_a_b, gate_x_w,
                     gate_x_b, lam, q_norm, w_uq, kv_norm, w_ukv, w_out):
    proj = h @ w_in
    x_r, gate_r, c_q, c_kv, k_pe = jnp.split(proj, HY_SPLITS, axis=-1)
    y_a = rglru_branch(x_r, gate_r, conv_w, conv_b, gate_a_w, gate_a_b, gate_x_w, gate_x_b, lam)
    y_b = mla_branch(c_q, c_kv, k_pe, cos, sin, q_norm, w_uq, kv_norm, w_ukv)
    return jnp.concatenate([y_a, y_b], axis=-1) @ w_out


def chunk_gated_delta_rule(q, k, v, g, beta):
    B, T, H, DK = q.shape
    DV = v.shape[-1]
    N = T // GDN_CHUNK
    f32 = jnp.float32

    def chunks(t):
        return t.astype(f32).reshape(B, N, GDN_CHUNK, H, -1).transpose(0, 1, 3, 2, 4)

    qc, kc, vc = chunks(q), chunks(k), chunks(v)
    gch = chunks(g[..., None])[..., 0]
    bch = chunks(beta[..., None])[..., 0]
    gc = jnp.cumsum(gch, axis=-1)
    idx = jnp.arange(GDN_CHUNK)
    causal = idx[:, None] >= idx[None, :]
    strict = idx[:, None] > idx[None, :]
    decay = jnp.exp(jnp.where(causal, gc[..., :, None] - gc[..., None, :], -jnp.inf))
    k_beta = kc * bch[..., None]
    a_strict = jnp.where(strict, jnp.einsum('bnhid,bnhjd->bnhij', k_beta, kc) * decay, 0.0)
    lhs = a_strict + jnp.eye(GDN_CHUNK, dtype=f32)
    u = lax.linalg.triangular_solve(lhs, vc * bch[..., None], left_side=True,
                                    lower=True, unit_diagonal=True)
    w = lax.linalg.triangular_solve(lhs, k_beta * jnp.exp(gc)[..., None], left_side=True,
                                    lower=True, unit_diagonal=True)
    qk = jnp.einsum('bnhid,bnhjd->bnhij', qc, kc) * decay
    q_dec = qc * jnp.exp(gc)[..., None]
    k_dec = kc * jnp.exp(gc[..., -1:] - gc)[..., None]
    chunk_decay = jnp.exp(gc[..., -1])

    def step(S, xs):
        u_n, w_n, qk_n, q_n, k_n, d_n = xs
        v_new = u_n - jnp.einsum('bhck,bhkv->bhcv', w_n, S)
        o_n = jnp.einsum('bhck,bhkv->bhcv', q_n, S) + jnp.einsum('bhij,bhjv->bhiv', qk_n, v_new)
        S = S * d_n[..., None, None] + jnp.einsum('bhck,bhcv->bhkv', k_n, v_new)
        return S, o_n

    xs = tuple(jnp.moveaxis(t, 1, 0) for t in (u, w, qk, q_dec, k_dec, chunk_decay))
    S0 = jnp.zeros((B, H, DK, DV), f32)
    _, o = lax.scan(step, S0, xs)
    return o.transpose(1, 0, 3, 2, 4).reshape(B, T, H, DV).astype(v.dtype)


def gated_delta_net(h, w_in, conv_w, a_log, dt_bias, norm_w, w_out):
    B, T, _ = h.shape
    proj = h @ w_in
    qkv, z, b, a = jnp.split(proj, GDN_SPLITS, axis=-1)
    qkv = jax.nn.silu(causal_depthwise_conv(qkv, conv_w))
    q, k, v = jnp.split(qkv, [GDN_QK_W, 2 * GDN_QK_W], axis=-1)
    rep = GDN_V_HEADS // GDN_K_HEADS
    q = l2_norm(q.reshape(B, T, GDN_K_HEADS, GDN_HEAD_DIM)) * GDN_HEAD_DIM ** -0.5
    k = l2_norm(k.reshape(B, T, GDN_K_HEADS, GDN_HEAD_DIM))
    q = jnp.repeat(q, rep, axis=2)
    k = jnp.repeat(k, rep, axis=2)
    v = v.reshape(B, T, GDN_V_HEADS, GDN_HEAD_DIM)
    beta = jax.nn.sigmoid(b.astype(jnp.float32))
    g = -jnp.exp(a_log.astype(jnp.float32)) * jax.nn.softplus(
        a.astype(jnp.float32) + dt_bias.astype(jnp.float32))
    o = chunk_gated_delta_rule(q, k, v, g, beta)
    o = rms_norm(o, norm_w) * jax.nn.silu(z.reshape(B, T, GDN_V_HEADS, GDN_HEAD_DIM))
    return o.reshape(B, T, GDN_V_W) @ w_out


def swiglu(h, w_gate, w_up, w_down):
    return (jax.nn.silu(h @ w_gate) * (h @ w_up)) @ w_down


def setup_inputs(seed: int = 0) -> dict:
    key = jax.random.key(seed)
    ks = list(jax.random.split(key, 32))
    f32 = jnp.float32

    def normal(k, shape, fan_in):
        return jax.random.normal(k, shape, f32) * fan_in ** -0.5

    def gain(k, shape):
        return 1.0 + 0.02 * jax.random.normal(k, shape, f32)

    def bias(k, shape):
        return 0.01 * jax.random.normal(k, shape, f32)

    x = jax.random.normal(ks[0], (BATCH, SEQ, D_MODEL), f32)
    offset = jax.random.randint(ks[1], (BATCH, 1), 0, 4096, dtype=jnp.int32)
    positions = offset + jnp.arange(SEQ, dtype=jnp.int32)[None, :]

    a_c = jax.random.uniform(ks[12], (N_EVEN, RG_W), f32, 0.9, 0.999)
    s = a_c ** (1.0 / RG_C)
    rg_lambda = jnp.log(s) - jnp.log1p(-s)

    a_log = jnp.log(jax.random.uniform(ks[20], (N_ODD, GDN_V_HEADS), f32, 1.0, 16.0))
    dt = jnp.exp(jax.random.uniform(ks[21], (N_ODD, GDN_V_HEADS), f32,
                                    math.log(1e-3), math.log(1e-1)))
    dt_bias = dt + jnp.log(-jnp.expm1(-dt))

    return {
        "x": x,
        "positions": positions,
        "norm_mix_pre": gain(ks[2], (DEPTH, D_MODEL)),
        "norm_mix_post": gain(ks[3], (DEPTH, D_MODEL)),
        "norm_ffn_pre": gain(ks[4], (DEPTH, D_MODEL)),
        "norm_ffn_post": gain(ks[5], (DEPTH, D_MODEL)),
        "hy_w_in": normal(ks[6], (N_EVEN, D_MODEL, HY_IN), D_MODEL),
        "rg_conv_w": normal(ks[7], (N_EVEN, CONV_WIDTH, RG_W), CONV_WIDTH),
        "rg_conv_b": bias(ks[8], (N_EVEN, RG_W)),
        "rg_gate_a_w": normal(ks[9], (N_EVEN, RG_BLOCKS, RG_BLOCK_W, RG_BLOCK_W), RG_BLOCK_W),
        "rg_gate_a_b": bias(ks[10], (N_EVEN, RG_W)),
        "rg_gate_x_w": normal(ks[11], (N_EVEN, RG_BLOCKS, RG_BLOCK_W, RG_BLOCK_W), RG_BLOCK_W),
        "rg_gate_x_b": bias(ks[13], (N_EVEN, RG_W)),
        "rg_lambda": rg_lambda,
        "mla_q_norm": gain(ks[14], (N_EVEN, Q_LORA)),
        "mla_w_uq": normal(ks[15], (N_EVEN, Q_LORA, MLA_HEADS * MLA_QK), Q_LORA),
        "mla_kv_norm": gain(ks[16], (N_EVEN, KV_LORA)),
        "mla_w_ukv": normal(ks[17], (N_EVEN, KV_LORA, MLA_HEADS * (MLA_NOPE + MLA_V)), KV_LORA),
        "hy_w_out": normal(ks[18], (N_EVEN, HY_MIX, D_MODEL), HY_MIX),
        "gdn_w_in": normal(ks[19], (N_ODD, D_MODEL, GDN_IN), D_MODEL),
        "gdn_conv_w": normal(ks[22], (N_ODD, CONV_WIDTH, GDN_CONV_W), CONV_WIDTH),
        "gdn_a_log": a_log,
        "gdn_dt_bias": dt_bias,
        "gdn_norm": gain(ks[23], (N_ODD, GDN_HEAD_DIM)),
        "gdn_w_out": normal(ks[24], (N_ODD, GDN_V_W, D_MODEL), GDN_V_W),
        "ffn_w_gate": normal(ks[25], (DEPTH, D_MODEL, FFN_HIDDEN), D_MODEL),
        "ffn_w_up": normal(ks[26], (DEPTH, D_MODEL, FFN_HIDDEN), D_MODEL),
        "ffn_w_down": normal(ks[27], (DEPTH, FFN_HIDDEN, D_MODEL), FFN_HIDDEN),
    }


def reference(x, positions, norm_mix_pre, norm_mix_post, norm_ffn_pre, norm_ffn_post,
              hy_w_in, rg_conv_w, rg_conv_b, rg_gate_a_w, rg_gate_a_b, rg_gate_x_w,
              rg_gate_x_b, rg_lambda, mla_q_norm, mla_w_uq, mla_kv_norm, mla_w_ukv, hy_w_out,
              gdn_w_in, gdn_conv_w, gdn_a_log, gdn_dt_bias, gdn_norm, gdn_w_out,
              ffn_w_gate, ffn_w_up, ffn_w_down):
    cos, sin = rope_tables(positions)
    for layer in range(DEPTH):
        i = layer // 2
        h = rms_norm(x, norm_mix_pre[layer])
        if layer % 2 == 0:
            m = hybrid_rglru_mla(h, cos, sin, hy_w_in[i], rg_conv_w[i], rg_conv_b[i],
                                 rg_gate_a_w[i], rg_gate_a_b[i], rg_gate_x_w[i], rg_gate_x_b[i],
                                 rg_lambda[i], mla_q_norm[i], mla_w_uq[i], mla_kv_norm[i],
                                 mla_w_ukv[i], hy_w_out[i])
        else:
            m = gated_delta_net(h, gdn_w_in[i], gdn_conv_w[i], gdn_a_log[i], gdn_dt_bias[i],
                                gdn_norm[i], gdn_w_out[i])
        x = x + rms_norm(m, norm_mix_post[layer])
        h = rms_norm(x, norm_ffn_pre[layer])
        x = x + rms_norm(swiglu(h, ffn_w_gate[layer], ffn_w_up[layer], ffn_w_down[layer]),
                         norm_ffn_post[layer])
    return x
```

```python
import functools

import jax
import jax.numpy as jnp
from jax import lax
from jax.experimental import pallas as pl
from jax.experimental.pallas import tpu as pltpu

F32 = jnp.float32
BF16 = jnp.bfloat16

NORM_EPS = 1e-6
LANES = 128
SUBLANES = 8
VMEM_LIMIT_BYTES = 48 * 1024 * 1024

CONV_WIDTH = 4
RG_BLOCKS = 8
RG_C = 8.0
MLA_HEADS = 8
MLA_NOPE = 128
MLA_ROPE = 64
MLA_V = 128
MLA_QK = MLA_NOPE + MLA_ROPE
Q_LORA = 512
KV_LORA = 256
ROPE_THETA = 10000.0
GDN_K_HEADS = 8
GDN_V_HEADS = 16
GDN_HEAD_DIM = 128
GDN_CHUNK = 64
GDN_SOLVE_BLOCK = 16

ROW_TILE = 512
ATTN_TILE = 256
RG_TIME_TILE = 256
GDN_TIME_TILE = 256


def _params(*semantics):
    return pltpu.CompilerParams(dimension_semantics=semantics, vmem_limit_bytes=VMEM_LIMIT_BYTES)


def _rms(x, w):
    return x * lax.rsqrt(jnp.mean(x * x, axis=-1, keepdims=True) + NORM_EPS) * w


def _mm(a, b):
    return jnp.dot(a.astype(BF16), b.astype(BF16), preferred_element_type=F32)


def _mm_f32(a, b):
    return jnp.dot(a, b, preferred_element_type=F32, precision=lax.Precision.HIGHEST)


def _mm_nt(a, b):
    return lax.dot_general(a.astype(BF16), b.astype(BF16), (((1,), (1,)), ((), ())),
                           preferred_element_type=F32)


def _mm_tn(a, b):
    return lax.dot_general(a.astype(BF16), b.astype(BF16), (((0,), (0,)), ((), ())),
                           preferred_element_type=F32)


def _resident(shape):
    zeros = (0,) * len(shape)
    return pl.BlockSpec(shape, lambda *_: zeros)


def _shift_rows(x, prev_tail, s):
    xs = pltpu.roll(x, s, axis=0)
    ps = pltpu.roll(prev_tail, s, axis=0)
    row = lax.broadcasted_iota(jnp.int32, (SUBLANES, x.shape[1]), 0)
    first = jnp.where(row < s, ps, xs[:SUBLANES])
    return jnp.concatenate([first, xs[SUBLANES:]], axis=0)


def _causal_conv(x, prev_tail, w):
    y = x * w[CONV_WIDTH - 1:CONV_WIDTH]
    for s in range(1, CONV_WIDTH):
        y = y + _shift_rows(x, prev_tail, s) * w[CONV_WIDTH - 1 - s:CONV_WIDTH - s]
    return y


def _silu(x):
    return x * jax.nn.sigmoid(x)


def _softplus(x):
    return jnp.maximum(x, 0.0) + jnp.log1p(jnp.exp(-jnp.abs(x)))


def _norm_proj_kernel(x_ref, nw_ref, *refs, n_out):
    w_refs, o_refs = refs[:n_out], refs[n_out:]
    h = _rms(x_ref[...], nw_ref[...]).astype(BF16)
    for w_ref, o_ref in zip(w_refs, o_refs):
        o_ref[...] = jnp.dot(h, w_ref[...], preferred_element_type=F32).astype(o_ref.dtype)


def _norm_proj(x, nw, weights, out_dtypes):
    m, d = x.shape
    tm = min(ROW_TILE, m)
    n_out = len(weights)
    in_specs = [pl.BlockSpec((tm, d), lambda i: (i, 0)), _resident((1, d))]
    in_specs += [_resident(w.shape) for w in weights]
    out_specs = [pl.BlockSpec((tm, w.shape[1]), lambda i: (i, 0)) for w in weights]
    out_shape = [jax.ShapeDtypeStruct((m, w.shape[1]), dt) for w, dt in zip(weights, out_dtypes)]
    return pl.pallas_call(
        functools.partial(_norm_proj_kernel, n_out=n_out),
        grid=(m // tm,), in_specs=in_specs, out_specs=out_specs, out_shape=out_shape,
        compiler_params=_params("parallel"), name="norm_proj",
    )(x, nw.reshape(1, d), *weights)


def _out_proj_kernel(*refs, n_in):
    a_refs, w_refs = refs[:n_in], refs[n_in:2 * n_in]
    x_ref, nw_ref, o_ref = refs[2 * n_in:]
    acc = jnp.dot(a_refs[0][...], w_refs[0][...], preferred_element_type=F32)
    for a_ref, w_ref in zip(a_refs[1:], w_refs[1:]):
        acc = acc + jnp.dot(a_ref[...], w_ref[...], preferred_element_type=F32)
    o_ref[...] = x_ref[...] + _rms(acc, nw_ref[...])


def _out_proj(acts, weights, x, nw):
    m, d = x.shape
    tm = min(ROW_TILE, m)
    n_in = len(acts)
    in_specs = [pl.BlockSpec((tm, a.shape[1]), lambda i: (i, 0)) for a in acts]
    in_specs += [_resident(w.shape) for w in weights]
    in_specs += [pl.BlockSpec((tm, d), lambda i: (i, 0)), _resident((1, d))]
    return pl.pallas_call(
        functools.partial(_out_proj_kernel, n_in=n_in),
        grid=(m // tm,), in_specs=in_specs,
        out_specs=pl.BlockSpec((tm, d), lambda i: (i, 0)),
        out_shape=jax.ShapeDtypeStruct((m, d), F32),
        input_output_aliases={2 * n_in: 0},
        compiler_params=_params("parallel"), name="out_proj",
    )(*acts, *weights, x, nw.reshape(1, d))


def _ffn_kernel(x_ref, npre_ref, wg_ref, wu_ref, wd_ref, npost_ref, o_ref, *, hidden_tile):
    x = x_ref[...]
    h = _rms(x, npre_ref[...]).astype(BF16)
    hidden = wg_ref.shape[1]
    acc = None
    for c0 in range(0, hidden, hidden_tile):
        g = jnp.dot(h, wg_ref[:, c0:c0 + hidden_tile], preferred_element_type=F32)
        u = jnp.dot(h, wu_ref[:, c0:c0 + hidden_tile], preferred_element_type=F32)
        a = (_silu(g) * u).astype(BF16)
        part = jnp.dot(a, wd_ref[c0:c0 + hidden_tile, :], preferred_element_type=F32)
        acc = part if acc is None else acc + part
    o_ref[...] = x + _rms(acc, npost_ref[...])


def _ffn(x, npre, wg, wu, wd, npost):
    m, d = x.shape
    tm = min(ROW_TILE, m)
    hidden = wg.shape[1]
    hidden_tile = hidden // 2 if (hidden // 2) % LANES == 0 else hidden
    return pl.pallas_call(
        functools.partial(_ffn_kernel, hidden_tile=hidden_tile),
        grid=(m // tm,),
        in_specs=[pl.BlockSpec((tm, d), lambda i: (i, 0)), _resident((1, d)),
                  _resident(wg.shape), _resident(wu.shape), _resident(wd.shape), _resident((1, d))],
        out_specs=pl.BlockSpec((tm, d), lambda i: (i, 0)),
        out_shape=jax.ShapeDtypeStruct((m, d), F32),
        input_output_aliases={0: 0},
        compiler_params=_params("parallel"), name="ffn",
    )(x, npre.reshape(1, d), wg, wu, wd, npost.reshape(1, d))


def _rope_table_kernel(pos_ref, invf_ref, cos_ref, sin_ref):
    ang = pos_ref[...].astype(F32) * invf_ref[...]
    cos_ref[...] = jnp.cos(ang)
    sin_ref[...] = jnp.sin(ang)


def _rope_tables(positions):
    m = positions.size
    tm = min(ROW_TILE, m)
    half = MLA_ROPE // 2
    inv_freq = 1.0 / (ROPE_THETA ** (jnp.arange(0, MLA_ROPE, 2, dtype=F32) / MLA_ROPE))
    invf = jnp.tile(inv_freq, LANES // half).reshape(1, LANES)
    return pl.pallas_call(
        _rope_table_kernel, grid=(m // tm,),
        in_specs=[pl.BlockSpec((tm, 1), lambda i: (i, 0)), _resident((1, LANES))],
        out_specs=[pl.BlockSpec((tm, LANES), lambda i: (i, 0))] * 2,
        out_shape=[jax.ShapeDtypeStruct((m, LANES), F32)] * 2,
        compiler_params=_params("parallel"), name="rope_tables",
    )(positions.reshape(m, 1), invf)


def _rglru_kernel(xr_ref, gr_ref, cw_ref, cb_ref, wa_ref, ba_ref, wx_ref, bx_ref, lam_ref, o_ref,
                  h_sc, tail_sc):
    @pl.when(pl.program_id(1) == 0)
    def _():
        h_sc[...] = jnp.zeros_like(h_sc)
        tail_sc[...] = jnp.zeros_like(tail_sc)

    x = xr_ref[...].astype(F32)
    t, width = x.shape
    xc = _causal_conv(x, tail_sc[...], cw_ref[...]) + cb_ref[...]
    tail_sc[...] = x[t - SUBLANES:]

    xcb = xc.astype(BF16)
    bw = width // RG_BLOCKS
    ra = jnp.concatenate([jnp.dot(xcb[:, n * bw:(n + 1) * bw], wa_ref[n], preferred_element_type=F32)
                          for n in range(RG_BLOCKS)], axis=1)
    rx = jnp.concatenate([jnp.dot(xcb[:, n * bw:(n + 1) * bw], wx_ref[n], preferred_element_type=F32)
                          for n in range(RG_BLOCKS)], axis=1)
    r = jax.nn.sigmoid(ra + ba_ref[...])
    i = jax.nn.sigmoid(rx + bx_ref[...])
    log_a = (-RG_C * _softplus(-lam_ref[...])) * r
    a = jnp.exp(log_a)
    th = jnp.tanh(log_a)
    b = jnp.sqrt(-2.0 * th / (1.0 - th)) * (i * xc)

    row = lax.broadcasted_iota(jnp.int32, (t, width), 0)
    s = 1
    while s < t:
        a_s = pltpu.roll(a, s, axis=0)
        b_s = pltpu.roll(b, s, axis=0)
        keep = row >= s
        b = jnp.where(keep, a * b_s + b, b)
        a = jnp.where(keep, a * a_s, a)
        s *= 2
    h = b + a * h_sc[0:1]
    h_sc[...] = jnp.broadcast_to(h[t - 1:t], h_sc.shape)
    o_ref[...] = (h * jax.nn.gelu(gr_ref[...].astype(F32))).astype(o_ref.dtype)


def _rglru(x_r, gate_r, conv_w, conv_b, wa, ba, wx, bx, lam, batch, seq):
    width = x_r.shape[-1]
    tt = min(RG_TIME_TILE, seq)
    x_r = x_r.reshape(batch, seq, width)
    gate_r = gate_r.reshape(batch, seq, width)
    blk = pl.BlockSpec((None, tt, width), lambda b, t: (b, t, 0))
    vec = _resident((1, width))
    out = pl.pallas_call(
        _rglru_kernel, grid=(batch, seq // tt),
        in_specs=[blk, blk, _resident(conv_w.shape), vec, _resident(wa.shape), vec,
                  _resident(wx.shape), vec, vec],
        out_specs=blk,
        out_shape=jax.ShapeDtypeStruct((batch, seq, width), BF16),
        scratch_shapes=[pltpu.VMEM((SUBLANES, width), F32), pltpu.VMEM((SUBLANES, width), F32)],
        compiler_params=_params("parallel", "arbitrary"), name="rglru",
    )(x_r, gate_r, conv_w, conv_b.reshape(1, width), wa, ba.reshape(1, width), wx,
      bx.reshape(1, width), lam.reshape(1, width))
    return out.reshape(batch * seq, width)


def _mla_proj_kernel(cq_ref, ckv_ref, cos_ref, sin_ref, qnw_ref, kvnw_ref, wqn_ref, wqr_ref, wqrr_ref,
                     wk_ref, wv_ref, qn_o, qr_o, kn_o, v_o, kr_o):
    scale = MLA_QK ** -0.5
    qn = _rms(cq_ref[...].astype(F32), qnw_ref[...]).astype(BF16)
    qn_o[...] = (jnp.dot(qn, wqn_ref[...], preferred_element_type=F32) * scale).astype(qn_o.dtype)
    cos = cos_ref[...]
    sin = sin_ref[...]
    reps = qr_o.shape[1] // LANES
    cos_q = jnp.concatenate([cos] * reps, axis=1)
    sin_q = jnp.concatenate([sin] * reps, axis=1)
    qr = jnp.dot(qn, wqr_ref[...], preferred_element_type=F32)
    qr_rot = jnp.dot(qn, wqrr_ref[...], preferred_element_type=F32)
    qr_o[...] = ((qr * cos_q + qr_rot * sin_q) * scale).astype(qr_o.dtype)

    ckv = ckv_ref[...]
    kvn = _rms(ckv[:, :KV_LORA].astype(F32), kvnw_ref[...]).astype(BF16)
    kn_o[...] = jnp.dot(kvn, wk_ref[...], preferred_element_type=F32).astype(kn_o.dtype)
    v_o[...] = jnp.dot(kvn, wv_ref[...], preferred_element_type=F32).astype(v_o.dtype)
    k_pe = ckv[:, KV_LORA:KV_LORA + MLA_ROPE].astype(F32)
    k_pe_rot = ckv[:, KV_LORA + MLA_ROPE:].astype(F32)
    kr_o[...] = (k_pe * cos[:, :MLA_ROPE] + k_pe_rot * sin[:, :MLA_ROPE]).astype(kr_o.dtype)


def _mla_proj(c_q, ckv, cos, sin, q_norm, kv_norm, wqn, wqr, wqrr, wk, wv):
    m = c_q.shape[0]
    tm = min(ROW_TILE, m)
    row = lambda n: pl.BlockSpec((tm, n), lambda i: (i, 0))
    outs = [wqn.shape[1], wqr.shape[1], wk.shape[1], wv.shape[1], MLA_ROPE]
    return pl.pallas_call(
        _mla_proj_kernel, grid=(m // tm,),
        in_specs=[row(c_q.shape[1]), row(ckv.shape[1]), row(LANES), row(LANES),
                  _resident((1, Q_LORA)), _resident((1, KV_LORA)), _resident(wqn.shape),
                  _resident(wqr.shape), _resident(wqrr.shape), _resident(wk.shape), _resident(wv.shape)],
        out_specs=[row(n) for n in outs],
        out_shape=[jax.ShapeDtypeStruct((m, n), BF16) for n in outs],
        compiler_params=_params("parallel"), name="mla_proj",
    )(c_q, ckv, cos, sin, q_norm.reshape(1, Q_LORA), kv_norm.reshape(1, KV_LORA), wqn, wqr, wqrr, wk, wv)


ATTN_HEADS_PER_STEP = 2


def _attn_kernel(qn_ref, qr_ref, kn_ref, kr_ref, v_ref, o_ref, *, tile):
    qi = pl.program_id(2)
    row = lax.broadcasted_iota(jnp.int32, (tile, tile), 0)
    col = lax.broadcasted_iota(jnp.int32, (tile, tile), 1)
    for hh in range(ATTN_HEADS_PER_STEP):
        qn = qn_ref[:, hh * MLA_NOPE:(hh + 1) * MLA_NOPE]
        qr = qr_ref[:, hh * MLA_ROPE:(hh + 1) * MLA_ROPE]

        def block(j, carry, masked):
            m_prev, l_prev, acc = carry
            start = pl.multiple_of(j * tile, tile)
            kn = kn_ref[pl.ds(start, tile), hh * MLA_NOPE:(hh + 1) * MLA_NOPE]
            kr = kr_ref[pl.ds(start, tile), :]
            vv = v_ref[pl.ds(start, tile), hh * MLA_V:(hh + 1) * MLA_V]
            s = _mm_nt(qn, kn) + _mm_nt(qr, kr)
            if masked:
                s = jnp.where(row >= col, s, -jnp.inf)
            m_new = jnp.maximum(m_prev, jnp.max(s, axis=-1, keepdims=True))
            alpha = jnp.exp(m_prev - m_new)
            p = jnp.exp(s - m_new)
            l_new = alpha * l_prev + jnp.sum(p, axis=-1, keepdims=True)
            acc = alpha * acc + jnp.dot(p.astype(BF16), vv, preferred_element_type=F32)
            return m_new, l_new, acc

        init = (jnp.full((tile, 1), -jnp.inf, F32), jnp.zeros((tile, 1), F32),
                jnp.zeros((tile, MLA_V), F32))
        carry = lax.fori_loop(0, qi, lambda j, c: block(j, c, False), init)
        _, l_fin, acc = block(qi, carry, True)
        o_ref[:, hh * MLA_V:(hh + 1) * MLA_V] = (acc / l_fin).astype(o_ref.dtype)


def _attention(q_nope, q_rope, k_nope, k_rope, v, batch, seq):
    tile = min(ATTN_TILE, seq)
    hp = ATTN_HEADS_PER_STEP
    r3 = lambda a: a.reshape(batch, seq, a.shape[-1])
    q_blk = lambda n: pl.BlockSpec((None, tile, hp * n), lambda b, h, q: (b, q, h))
    kv_blk = lambda n: pl.BlockSpec((None, seq, hp * n), lambda b, h, q: (b, 0, h))
    out = pl.pallas_call(
        functools.partial(_attn_kernel, tile=tile),
        grid=(batch, MLA_HEADS // hp, seq // tile),
        in_specs=[q_blk(MLA_NOPE), q_blk(MLA_ROPE), kv_blk(MLA_NOPE),
                  pl.BlockSpec((None, seq, MLA_ROPE), lambda b, h, q: (b, 0, 0)), kv_blk(MLA_V)],
        out_specs=q_blk(MLA_V),
        out_shape=jax.ShapeDtypeStruct((batch, seq, MLA_HEADS * MLA_V), BF16),
        compiler_params=_params("parallel", "parallel", "arbitrary"), name="mla_attention",
    )(r3(q_nope), r3(q_rope), r3(k_nope), r3(k_rope), r3(v))
    return out.reshape(batch * seq, MLA_HEADS * MLA_V)


def _unit_lower_inverse(a_strict):
    c = a_strict.shape[0]
    assert c == 4 * GDN_SOLVE_BLOCK
    r = lax.broadcasted_iota(jnp.int32, (c, c), 0)
    q = lax.broadcasted_iota(jnp.int32, (c, c), 1)
    eye = (r == q).astype(F32)
    same_block = (r // GDN_SOLVE_BLOCK) == (q // GDN_SOLVE_BLOCK)
    d = jnp.where(same_block, a_strict, 0.0)
    low = a_strict - d
    x = eye - d
    p = _mm_f32(d, d)
    x = x + _mm_f32(x, p)
    p = _mm_f32(p, p)
    x = x + _mm_f32(x, p)
    p = _mm_f32(p, p)
    x = x + _mm_f32(x, p)
    n = _mm_f32(x, low)
    y = eye - n
    y = y + _mm_f32(y, _mm_f32(n, n))
    return _mm_f32(y, x)


def _gdn_kernel(q_ref, k_ref, v_ref, z_ref, ba_ref, cwq_ref, cwk_ref, cwv_ref, alog_ref, dtb_ref, nw_ref,
                o_ref, s_sc, tail_sc):
    kh = pl.program_id(1)

    @pl.when(pl.program_id(2) == 0)
    def _():
        s_sc[...] = jnp.zeros_like(s_sc)
        tail_sc[...] = jnp.zeros_like(tail_sc)

    dk = GDN_HEAD_DIM
    rep = GDN_V_HEADS // GDN_K_HEADS
    c = GDN_CHUNK
    qkv = jnp.concatenate([q_ref[...], k_ref[...], v_ref[...]], axis=1).astype(F32)
    t = qkv.shape[0]
    cw = jnp.concatenate([cwq_ref[...], cwk_ref[...], cwv_ref[...]], axis=1)
    act = _silu(_causal_conv(qkv, tail_sc[...], cw))
    tail_sc[...] = qkv[t - SUBLANES:]
    q = act[:, :dk]
    k = act[:, dk:2 * dk]
    q = q * lax.rsqrt(jnp.sum(q * q, axis=-1, keepdims=True) + NORM_EPS) * (dk ** -0.5)
    k = k * lax.rsqrt(jnp.sum(k * k, axis=-1, keepdims=True) + NORM_EPS)

    ba = ba_ref[...]
    lane = lax.broadcasted_iota(jnp.int32, ba.shape, 1)
    plane = lax.broadcasted_iota(jnp.int32, alog_ref.shape, 1)
    sig_b = jax.nn.sigmoid(ba)

    ri = lax.broadcasted_iota(jnp.int32, (c, c), 0)
    ci = lax.broadcasted_iota(jnp.int32, (c, c), 1)
    incl = (ci <= ri).astype(F32)
    after = (ri > ci).astype(F32)
    lower = ri >= ci
    strict = ri > ci

    for vh in range(rep):
        h = kh * rep + vh
        beta_col = jnp.sum(jnp.where(lane == h, sig_b, 0.0), axis=-1, keepdims=True)
        a_col = jnp.sum(jnp.where(lane == GDN_V_HEADS + h, ba, 0.0), axis=-1, keepdims=True)
        a_log = jnp.sum(jnp.where(plane == h, alog_ref[...], 0.0), axis=-1, keepdims=True)
        dt_b = jnp.sum(jnp.where(plane == h, dtb_ref[...], 0.0), axis=-1, keepdims=True)
        g_col = -jnp.exp(a_log) * _softplus(a_col + dt_b)
        v_h = act[:, (2 + vh) * dk:(3 + vh) * dk]
        z_h = z_ref[:, vh * dk:(vh + 1) * dk].astype(F32)
        state = s_sc[vh]
        for n in range(t // c):
            rows = slice(n * c, (n + 1) * c)
            qn, kn, vn = q[rows], k[rows], v_h[rows]
            g_b = jnp.broadcast_to(g_col[rows], (c, dk))
            beta_b = jnp.broadcast_to(beta_col[rows], (c, dk))
            gc = _mm_f32(incl, g_b)
            diff = _mm_f32(incl, g_b[:, :c] * after)
            decay = jnp.where(lower, jnp.exp(jnp.where(lower, diff, 0.0)), 0.0)
            egc = jnp.exp(gc)
            gc_last = gc[c - 1:c]
            k_beta = kn * beta_b
            kk = _mm_nt(k_beta, kn)
            a_mat = jnp.where(strict, kk * decay, 0.0)
            t_inv = _unit_lower_inverse(a_mat)
            uw = _mm_f32(t_inv, jnp.concatenate([vn * beta_b, k_beta * egc], axis=1))
            u, w = uw[:, :dk], uw[:, dk:]
            ws_qs = _mm(jnp.concatenate([w, qn * egc], axis=0), state)
            v_new = u - ws_qs[:c]
            qk = _mm_nt(qn, kn) * decay
            o = ws_qs[c:] + _mm(qk, v_new)
            state = state * jnp.exp(gc_last) + _mm_tn(kn * jnp.exp(gc_last - gc), v_new)
            o = _rms(o, nw_ref[...]) * _silu(z_h[rows])
            o_ref[rows, vh * dk:(vh + 1) * dk] = o.astype(o_ref.dtype)
        s_sc[vh] = state


def _gdn_core(qkv, z, ba, conv_w, a_log, dt_bias, norm_w, batch, seq):
    dk = GDN_HEAD_DIM
    rep = GDN_V_HEADS // GDN_K_HEADS
    tt = min(GDN_TIME_TILE, seq)
    qkv = qkv.reshape(batch, seq, qkv.shape[-1])
    z = z.reshape(batch, seq, z.shape[-1])
    ba = ba.reshape(batch, seq, ba.shape[-1])
    v_off = 2 * GDN_K_HEADS // rep
    blk = lambda n, off: pl.BlockSpec((None, tt, n), lambda b, h, t: (b, t, h + off))
    cblk = lambda n, off: pl.BlockSpec((CONV_WIDTH, n), lambda b, h, t: (0, h + off))
    out = pl.pallas_call(
        _gdn_kernel, grid=(batch, GDN_K_HEADS, seq // tt),
        in_specs=[blk(dk, 0), blk(dk, GDN_K_HEADS), blk(rep * dk, v_off), blk(rep * dk, 0),
                  pl.BlockSpec((None, tt, 2 * GDN_V_HEADS), lambda b, h, t: (b, t, 0)),
                  cblk(dk, 0), cblk(dk, GDN_K_HEADS), cblk(rep * dk, v_off),
                  _resident((1, GDN_V_HEADS)), _resident((1, GDN_V_HEADS)), _resident((1, dk))],
        out_specs=blk(rep * dk, 0),
        out_shape=jax.ShapeDtypeStruct((batch, seq, GDN_V_HEADS * dk), BF16),
        scratch_shapes=[pltpu.VMEM((rep, dk, dk), F32), pltpu.VMEM((SUBLANES, (2 + rep) * dk), F32)],
        compiler_params=_params("parallel", "parallel", "arbitrary"), name="gdn_core",
    )(qkv, qkv, qkv, z, ba, conv_w, conv_w, conv_w, a_log.reshape(1, -1), dt_bias.reshape(1, -1),
      norm_w.reshape(1, dk))
    return out.reshape(batch * seq, GDN_V_HEADS * dk)


def _rotate_half_columns(w, heads):
    k = w.shape[0]
    w = w.reshape(k, heads, 2, MLA_ROPE // 2)
    return jnp.concatenate([-w[:, :, 1:], w[:, :, :1]], axis=2).reshape(k, heads * MLA_ROPE)


def _hybrid_layer(x, cos, sin, npre, npost, w_in, conv_w, conv_b, gate_a_w, gate_a_b, gate_x_w, gate_x_b,
                  lam, q_norm, w_uq, kv_norm, w_ukv, w_out, batch, seq):
    rg_w = lam.shape[0]
    o1, o2, o3, o4 = rg_w, 2 * rg_w, 2 * rg_w + Q_LORA, 2 * rg_w + Q_LORA + KV_LORA
    w_kpe = w_in[:, o4:]
    w_ckv = jnp.concatenate([w_in[:, o3:o4], w_kpe, _rotate_half_columns(w_kpe, 1)], axis=1)
    weights = [w_in[:, :o1], w_in[:, o1:o2], w_in[:, o2:o3], w_ckv]
    x_r, gate_r, c_q, ckv = _norm_proj(x, npre, [w.astype(BF16) for w in weights], [BF16] * 4)

    y_a = _rglru(x_r, gate_r, conv_w, conv_b, gate_a_w.astype(BF16), gate_a_b, gate_x_w.astype(BF16),
                 gate_x_b, lam, batch, seq)

    wq = w_uq.reshape(Q_LORA, MLA_HEADS, MLA_QK)
    wqn = wq[:, :, :MLA_NOPE].reshape(Q_LORA, MLA_HEADS * MLA_NOPE)
    wqr = wq[:, :, MLA_NOPE:].reshape(Q_LORA, MLA_HEADS * MLA_ROPE)
    wkv = w_ukv.reshape(KV_LORA, MLA_HEADS, MLA_NOPE + MLA_V)
    wk = wkv[:, :, :MLA_NOPE].reshape(KV_LORA, MLA_HEADS * MLA_NOPE)
    wv = wkv[:, :, MLA_NOPE:].reshape(KV_LORA, MLA_HEADS * MLA_V)
    q_nope, q_rope, k_nope, v, k_rope = _mla_proj(
        c_q, ckv, cos, sin, q_norm, kv_norm, wqn.astype(BF16), wqr.astype(BF16),
        _rotate_half_columns(wqr, MLA_HEADS).astype(BF16), wk.astype(BF16), wv.astype(BF16))
    y_b = _attention(q_nope, q_rope, k_nope, k_rope, v, batch, seq)

    w_out = w_out.astype(BF16)
    return _out_proj([y_a, y_b], [w_out[:rg_w], w_out[rg_w:]], x, npost)


def _gdn_layer(x, npre, npost, w_in, conv_w, a_log, dt_bias, norm_w, w_out, batch, seq):
    conv_c = conv_w.shape[1]
    v_w = GDN_V_HEADS * GDN_HEAD_DIM
    weights = [w_in[:, :conv_c].astype(BF16), w_in[:, conv_c:conv_c + v_w].astype(BF16),
               w_in[:, conv_c + v_w:].astype(BF16)]
    qkv, z, ba = _norm_proj(x, npre, weights, [BF16, BF16, F32])
    y = _gdn_core(qkv, z, ba, conv_w, a_log, dt_bias, norm_w, batch, seq)
    return _out_proj([y], [w_out.astype(BF16)], x, npost)


def kernel(x, positions, norm_mix_pre, norm_mix_post, norm_ffn_pre, norm_ffn_post,
           hy_w_in, rg_conv_w, rg_conv_b, rg_gate_a_w, rg_gate_a_b, rg_gate_x_w,
           rg_gate_x_b, rg_lambda, mla_q_norm, mla_w_uq, mla_kv_norm, mla_w_ukv, hy_w_out,
           gdn_w_in, gdn_conv_w, gdn_a_log, gdn_dt_bias, gdn_norm, gdn_w_out,
           ffn_w_gate, ffn_w_up, ffn_w_down):
    batch, seq, d = x.shape
    depth = norm_mix_pre.shape[0]
    cos, sin = _rope_tables(positions)
    x = x.reshape(batch * seq, d)
    for layer in range(depth):
        i = layer // 2
        if layer % 2 == 0:
            x = _hybrid_layer(x, cos, sin, norm_mix_pre[layer], norm_mix_post[layer], hy_w_in[i],
                              rg_conv_w[i], rg_conv_b[i], rg_gate_a_w[i], rg_gate_a_b[i], rg_gate_x_w[i],
                              rg_gate_x_b[i], rg_lambda[i], mla_q_norm[i], mla_w_uq[i], mla_kv_norm[i],
                              mla_w_ukv[i], hy_w_out[i], batch, seq)
        else:
            x = _gdn_layer(x, norm_mix_pre[layer], norm_mix_post[layer], gdn_w_in[i], gdn_conv_w[i],
                           gdn_a_log[i], gdn_dt_bias[i], gdn_norm[i], gdn_w_out[i], batch, seq)
        x = _ffn(x, norm_ffn_pre[layer], ffn_w_gate[layer].astype(BF16), ffn_w_up[layer].astype(BF16),
                 ffn_w_down[layer].astype(BF16), norm_ffn_post[layer])
    return x.reshape(batch, seq, d)
```

```python
import functools

import jax
import jax.numpy as jnp
from jax import lax
from jax.experimental import pallas as pl
from jax.experimental.pallas import tpu as pltpu

F32 = jnp.float32
BF16 = jnp.bfloat16

NORM_EPS = 1e-6
LANES = 128
SUBLANES = 8
VMEM_LIMIT_BYTES = 48 * 1024 * 1024

CONV_WIDTH = 4
RG_BLOCKS = 8
RG_C = 8.0
MLA_HEADS = 8
MLA_NOPE = 128
MLA_ROPE = 64
MLA_V = 128
MLA_QK = MLA_NOPE + MLA_ROPE
Q_LORA = 512
KV_LORA = 256
ROPE_THETA = 10000.0
GDN_K_HEADS = 8
GDN_V_HEADS = 16
GDN_HEAD_DIM = 128
GDN_CHUNK_ROWS = 128
GDN_SOLVE_BLOCK = 16
GDN_HEAD_GROUP = 8

ROW_TILE = 512
ATTN_TILE = 256
RG_TIME_TILE = 256
GDN_TIME_TILE = 256


def _params(*semantics):
    return pltpu.CompilerParams(dimension_semantics=semantics, vmem_limit_bytes=VMEM_LIMIT_BYTES)


def _rms(x, w):
    return x * lax.rsqrt(jnp.mean(x * x, axis=-1, keepdims=True) + NORM_EPS) * w


def _mm(a, b):
    return jnp.dot(a.astype(BF16), b.astype(BF16), preferred_element_type=F32)


def _mm_f32(a, b):
    return jnp.dot(a, b, preferred_element_type=F32, precision=lax.Precision.HIGHEST)


def _mm_nt(a, b):
    return lax.dot_general(a.astype(BF16), b.astype(BF16), (((1,), (1,)), ((), ())),
                           preferred_element_type=F32)


def _mm_tn(a, b):
    return lax.dot_general(a.astype(BF16), b.astype(BF16), (((0,), (0,)), ((), ())),
                           preferred_element_type=F32)


def _resident(shape):
    zeros = (0,) * len(shape)
    return pl.BlockSpec(shape, lambda *_: zeros)


def _shift_rows(x, prev_tail, s):
    xs = pltpu.roll(x, s, axis=0)
    ps = pltpu.roll(prev_tail, s, axis=0)
    row = lax.broadcasted_iota(jnp.int32, (SUBLANES, x.shape[1]), 0)
    first = jnp.where(row < s, ps, xs[:SUBLANES])
    return jnp.concatenate([first, xs[SUBLANES:]], axis=0)


def _causal_conv(x, prev_tail, w):
    y = x * w[CONV_WIDTH - 1:CONV_WIDTH]
    for s in range(1, CONV_WIDTH):
        y = y + _shift_rows(x, prev_tail, s) * w[CONV_WIDTH - 1 - s:CONV_WIDTH - s]
    return y


def _silu(x):
    return x * jax.nn.sigmoid(x)


def _softplus(x):
    return jnp.maximum(x, 0.0) + jnp.log1p(jnp.exp(-jnp.abs(x)))


def _norm_proj_kernel(x_ref, nw_ref, *refs, n_out):
    w_refs, o_refs = refs[:n_out], refs[n_out:]
    h = _rms(x_ref[...], nw_ref[...]).astype(BF16)
    for w_ref, o_ref in zip(w_refs, o_refs):
        o_ref[...] = jnp.dot(h, w_ref[...], preferred_element_type=F32).astype(o_ref.dtype)


def _norm_proj(x, nw, weights, out_dtypes):
    m, d = x.shape
    tm = min(ROW_TILE, m)
    n_out = len(weights)
    in_specs = [pl.BlockSpec((tm, d), lambda i: (i, 0)), _resident((1, d))]
    in_specs += [_resident(w.shape) for w in weights]
    out_specs = [pl.BlockSpec((tm, w.shape[1]), lambda i: (i, 0)) for w in weights]
    out_shape = [jax.ShapeDtypeStruct((m, w.shape[1]), dt) for w, dt in zip(weights, out_dtypes)]
    return pl.pallas_call(
        functools.partial(_norm_proj_kernel, n_out=n_out),
        grid=(m // tm,), in_specs=in_specs, out_specs=out_specs, out_shape=out_shape,
        compiler_params=_params("parallel"), name="norm_proj",
    )(x, nw.reshape(1, d), *weights)


def _out_proj_kernel(*refs, n_in):
    a_refs, w_refs = refs[:n_in], refs[n_in:2 * n_in]
    x_ref, nw_ref, o_ref = refs[2 * n_in:]
    acc = jnp.dot(a_refs[0][...], w_refs[0][...], preferred_element_type=F32)
    for a_ref, w_ref in zip(a_refs[1:], w_refs[1:]):
        acc = acc + jnp.dot(a_ref[...], w_ref[...], preferred_element_type=F32)
    o_ref[...] = x_ref[...] + _rms(acc, nw_ref[...])


def _out_proj(acts, weights, x, nw):
    m, d = x.shape
    tm = min(ROW_TILE, m)
    n_in = len(acts)
    in_specs = [pl.BlockSpec((tm, a.shape[1]), lambda i: (i, 0)) for a in acts]
    in_specs += [_resident(w.shape) for w in weights]
    in_specs += [pl.BlockSpec((tm, d), lambda i: (i, 0)), _resident((1, d))]
    return pl.pallas_call(
        functools.partial(_out_proj_kernel, n_in=n_in),
        grid=(m // tm,), in_specs=in_specs,
        out_specs=pl.BlockSpec((tm, d), lambda i: (i, 0)),
        out_shape=jax.ShapeDtypeStruct((m, d), F32),
        input_output_aliases={2 * n_in: 0},
        compiler_params=_params("parallel"), name="out_proj",
    )(*acts, *weights, x, nw.reshape(1, d))


def _ffn_kernel(x_ref, npre_ref, wg_ref, wu_ref, wd_ref, npost_ref, o_ref, *, hidden_tile):
    x = x_ref[...]
    h = _rms(x, npre_ref[...]).astype(BF16)
    hidden = wg_ref.shape[1]
    acc = None
    for c0 in range(0, hidden, hidden_tile):
        g = jnp.dot(h, wg_ref[:, c0:c0 + hidden_tile], preferred_element_type=F32)
        u = jnp.dot(h, wu_ref[:, c0:c0 + hidden_tile], preferred_element_type=F32)
        a = (_silu(g) * u).astype(BF16)
        part = jnp.dot(a, wd_ref[c0:c0 + hidden_tile, :], preferred_element_type=F32)
        acc = part if acc is None else acc + part
    o_ref[...] = x + _rms(acc, npost_ref[...])


def _ffn(x, npre, wg, wu, wd, npost):
    m, d = x.shape
    tm = min(ROW_TILE, m)
    hidden = wg.shape[1]
    hidden_tile = hidden // 2 if (hidden // 2) % LANES == 0 else hidden
    return pl.pallas_call(
        functools.partial(_ffn_kernel, hidden_tile=hidden_tile),
        grid=(m // tm,),
        in_specs=[pl.BlockSpec((tm, d), lambda i: (i, 0)), _resident((1, d)),
                  _resident(wg.shape), _resident(wu.shape), _resident(wd.shape), _resident((1, d))],
        out_specs=pl.BlockSpec((tm, d), lambda i: (i, 0)),
        out_shape=jax.ShapeDtypeStruct((m, d), F32),
        input_output_aliases={0: 0},
        compiler_params=_params("parallel"), name="ffn",
    )(x, npre.reshape(1, d), wg, wu, wd, npost.reshape(1, d))


def _rope_table_kernel(pos_ref, invf_ref, cos_ref, sin_ref):
    ang = pos_ref[...].astype(F32) * invf_ref[...]
    cos_ref[...] = jnp.cos(ang)
    sin_ref[...] = jnp.sin(ang)


def _rope_tables(positions):
    m = positions.size
    tm = min(ROW_TILE, m)
    half = MLA_ROPE // 2
    inv_freq = 1.0 / (ROPE_THETA ** (jnp.arange(0, MLA_ROPE, 2, dtype=F32) / MLA_ROPE))
    invf = jnp.tile(inv_freq, LANES // half).reshape(1, LANES)
    return pl.pallas_call(
        _rope_table_kernel, grid=(m // tm,),
        in_specs=[pl.BlockSpec((tm, 1), lambda i: (i, 0)), _resident((1, LANES))],
        out_specs=[pl.BlockSpec((tm, LANES), lambda i: (i, 0))] * 2,
        out_shape=[jax.ShapeDtypeStruct((m, LANES), F32)] * 2,
        compiler_params=_params("parallel"), name="rope_tables",
    )(positions.reshape(m, 1), invf)


def _rglru_kernel(xr_ref, gr_ref, cw_ref, cb_ref, wa_ref, ba_ref, wx_ref, bx_ref, lam_ref, o_ref,
                  h_sc, tail_sc):
    @pl.when(pl.program_id(1) == 0)
    def _():
        h_sc[...] = jnp.zeros_like(h_sc)
        tail_sc[...] = jnp.zeros_like(tail_sc)

    x = xr_ref[...].astype(F32)
    t, width = x.shape
    xc = _causal_conv(x, tail_sc[...], cw_ref[...]) + cb_ref[...]
    tail_sc[...] = x[t - SUBLANES:]

    xcb = xc.astype(BF16)
    bw = width // RG_BLOCKS
    ra = jnp.concatenate([jnp.dot(xcb[:, n * bw:(n + 1) * bw], wa_ref[n], preferred_element_type=F32)
                          for n in range(RG_BLOCKS)], axis=1)
    rx = jnp.concatenate([jnp.dot(xcb[:, n * bw:(n + 1) * bw], wx_ref[n], preferred_element_type=F32)
                          for n in range(RG_BLOCKS)], axis=1)
    r = jax.nn.sigmoid(ra + ba_ref[...])
    i = jax.nn.sigmoid(rx + bx_ref[...])
    log_a = (-RG_C * _softplus(-lam_ref[...])) * r
    a = jnp.exp(log_a)
    th = jnp.tanh(log_a)
    b = jnp.sqrt(-2.0 * th / (1.0 - th)) * (i * xc)

    row = lax.broadcasted_iota(jnp.int32, (t, width), 0)
    s = 1
    while s < t:
        a_s = pltpu.roll(a, s, axis=0)
        b_s = pltpu.roll(b, s, axis=0)
        keep = row >= s
        b = jnp.where(keep, a * b_s + b, b)
        a = jnp.where(keep, a * a_s, a)
        s *= 2
    h = b + a * h_sc[0:1]
    h_sc[...] = jnp.broadcast_to(h[t - 1:t], h_sc.shape)
    o_ref[...] = (h * jax.nn.gelu(gr_ref[...].astype(F32))).astype(o_ref.dtype)


def _rglru(x_r, gate_r, conv_w, conv_b, wa, ba, wx, bx, lam, batch, seq):
    width = x_r.shape[-1]
    tt = min(RG_TIME_TILE, seq)
    x_r = x_r.reshape(batch, seq, width)
    gate_r = gate_r.reshape(batch, seq, width)
    blk = pl.BlockSpec((None, tt, width), lambda b, t: (b, t, 0))
    vec = _resident((1, width))
    out = pl.pallas_call(
        _rglru_kernel, grid=(batch, seq // tt),
        in_specs=[blk, blk, _resident(conv_w.shape), vec, _resident(wa.shape), vec,
                  _resident(wx.shape), vec, vec],
        out_specs=blk,
        out_shape=jax.ShapeDtypeStruct((batch, seq, width), BF16),
        scratch_shapes=[pltpu.VMEM((SUBLANES, width), F32), pltpu.VMEM((SUBLANES, width), F32)],
        compiler_params=_params("parallel", "arbitrary"), name="rglru",
    )(x_r, gate_r, conv_w, conv_b.reshape(1, width), wa, ba.reshape(1, width), wx,
      bx.reshape(1, width), lam.reshape(1, width))
    return out.reshape(batch * seq, width)


def _mla_proj_kernel(cq_ref, ckv_ref, cos_ref, sin_ref, qnw_ref, kvnw_ref, wqn_ref, wqr_ref, wqrr_ref,
                     wk_ref, wv_ref, qn_o, qr_o, kn_o, v_o, kr_o):
    scale = MLA_QK ** -0.5
    qn = _rms(cq_ref[...].astype(F32), qnw_ref[...]).astype(BF16)
    qn_o[...] = (jnp.dot(qn, wqn_ref[...], preferred_element_type=F32) * scale).astype(qn_o.dtype)
    cos = cos_ref[...]
    sin = sin_ref[...]
    reps = qr_o.shape[1] // LANES
    cos_q = jnp.concatenate([cos] * reps, axis=1)
    sin_q = jnp.concatenate([sin] * reps, axis=1)
    qr = jnp.dot(qn, wqr_ref[...], preferred_element_type=F32)
    qr_rot = jnp.dot(qn, wqrr_ref[...], preferred_element_type=F32)
    qr_o[...] = ((qr * cos_q + qr_rot * sin_q) * scale).astype(qr_o.dtype)

    ckv = ckv_ref[...]
    kvn = _rms(ckv[:, :KV_LORA].astype(F32), kvnw_ref[...]).astype(BF16)
    kn_o[...] = jnp.dot(kvn, wk_ref[...], preferred_element_type=F32).astype(kn_o.dtype)
    v_o[...] = jnp.dot(kvn, wv_ref[...], preferred_element_type=F32).astype(v_o.dtype)
    k_pe = ckv[:, KV_LORA:KV_LORA + MLA_ROPE].astype(F32)
    k_pe_rot = ckv[:, KV_LORA + MLA_ROPE:].astype(F32)
    kr_o[...] = (k_pe * cos[:, :MLA_ROPE] + k_pe_rot * sin[:, :MLA_ROPE]).astype(kr_o.dtype)


def _mla_proj(c_q, ckv, cos, sin, q_norm, kv_norm, wqn, wqr, wqrr, wk, wv):
    m = c_q.shape[0]
    tm = min(ROW_TILE, m)
    row = lambda n: pl.BlockSpec((tm, n), lambda i: (i, 0))
    outs = [wqn.shape[1], wqr.shape[1], wk.shape[1], wv.shape[1], MLA_ROPE]
    return pl.pallas_call(
        _mla_proj_kernel, grid=(m // tm,),
        in_specs=[row(c_q.shape[1]), row(ckv.shape[1]), row(LANES), row(LANES),
                  _resident((1, Q_LORA)), _resident((1, KV_LORA)), _resident(wqn.shape),
                  _resident(wqr.shape), _resident(wqrr.shape), _resident(wk.shape), _resident(wv.shape)],
        out_specs=[row(n) for n in outs],
        out_shape=[jax.ShapeDtypeStruct((m, n), BF16) for n in outs],
        compiler_params=_params("parallel"), name="mla_proj",
    )(c_q, ckv, cos, sin, q_norm.reshape(1, Q_LORA), kv_norm.reshape(1, KV_LORA), wqn, wqr, wqrr, wk, wv)


ATTN_HEADS_PER_STEP = 2


def _attn_kernel(qn_ref, qr_ref, kn_ref, kr_ref, v_ref, o_ref, *, tile):
    qi = pl.program_id(2)
    row = lax.broadcasted_iota(jnp.int32, (tile, tile), 0)
    col = lax.broadcasted_iota(jnp.int32, (tile, tile), 1)
    for hh in range(ATTN_HEADS_PER_STEP):
        qn = qn_ref[:, hh * MLA_NOPE:(hh + 1) * MLA_NOPE]
        qr = qr_ref[:, hh * MLA_ROPE:(hh + 1) * MLA_ROPE]

        def block(j, carry, masked):
            m_prev, l_prev, acc = carry
            start = pl.multiple_of(j * tile, tile)
            kn = kn_ref[pl.ds(start, tile), hh * MLA_NOPE:(hh + 1) * MLA_NOPE]
            kr = kr_ref[pl.ds(start, tile), :]
            vv = v_ref[pl.ds(start, tile), hh * MLA_V:(hh + 1) * MLA_V]
            s = _mm_nt(qn, kn) + _mm_nt(qr, kr)
            if masked:
                s = jnp.where(row >= col, s, -jnp.inf)
            m_new = jnp.maximum(m_prev, jnp.max(s, axis=-1, keepdims=True))
            alpha = jnp.exp(m_prev - m_new)
            p = jnp.exp(s - m_new)
            l_new = alpha * l_prev + jnp.sum(p, axis=-1, keepdims=True)
            acc = alpha * acc + jnp.dot(p.astype(BF16), vv, preferred_element_type=F32)
            return m_new, l_new, acc

        init = (jnp.full((tile, 1), -jnp.inf, F32), jnp.zeros((tile, 1), F32),
                jnp.zeros((tile, MLA_V), F32))
        carry = lax.fori_loop(0, qi, lambda j, c: block(j, c, False), init)
        _, l_fin, acc = block(qi, carry, True)
        o_ref[:, hh * MLA_V:(hh + 1) * MLA_V] = (acc / l_fin).astype(o_ref.dtype)


def _attention(q_nope, q_rope, k_nope, k_rope, v, batch, seq):
    tile = min(ATTN_TILE, seq)
    hp = ATTN_HEADS_PER_STEP
    r3 = lambda a: a.reshape(batch, seq, a.shape[-1])
    q_blk = lambda n: pl.BlockSpec((None, tile, hp * n), lambda b, h, q: (b, q, h))
    kv_blk = lambda n: pl.BlockSpec((None, seq, hp * n), lambda b, h, q: (b, 0, h))
    out = pl.pallas_call(
        functools.partial(_attn_kernel, tile=tile),
        grid=(batch, MLA_HEADS // hp, seq // tile),
        in_specs=[q_blk(MLA_NOPE), q_blk(MLA_ROPE), kv_blk(MLA_NOPE),
                  pl.BlockSpec((None, seq, MLA_ROPE), lambda b, h, q: (b, 0, 0)), kv_blk(MLA_V)],
        out_specs=q_blk(MLA_V),
        out_shape=jax.ShapeDtypeStruct((batch, seq, MLA_HEADS * MLA_V), BF16),
        compiler_params=_params("parallel", "parallel", "arbitrary"), name="mla_attention",
    )(r3(q_nope), r3(q_rope), r3(k_nope), r3(k_rope), r3(v))
    return out.reshape(batch * seq, MLA_HEADS * MLA_V)


def _nilpotent_inverse_batch(mats, order, eye):
    xs = [eye - m for m in mats]
    ps = mats
    k = 1
    while 2 * k < order:
        ps = [_mm(p, p) for p in ps]
        xs = [x + _mm(x, p) for x, p in zip(xs, ps)]
        k *= 2
    return xs


def _unit_lower_inverse_batch(mats):
    c = mats[0].shape[0]
    r = lax.broadcasted_iota(jnp.int32, (c, c), 0)
    q = lax.broadcasted_iota(jnp.int32, (c, c), 1)
    eye = (r == q).astype(F32)
    same_block = (r // GDN_SOLVE_BLOCK) == (q // GDN_SOLVE_BLOCK)
    ds = [jnp.where(same_block, a, 0.0) for a in mats]
    lows = [a - d for a, d in zip(mats, ds)]
    xs = _nilpotent_inverse_batch(ds, GDN_SOLVE_BLOCK, eye)
    ns = [_mm(x, low) for x, low in zip(xs, lows)]
    ys = _nilpotent_inverse_batch(ns, c // GDN_SOLVE_BLOCK, eye)
    return [_mm(y, x) for y, x in zip(ys, xs)]


def _gdn_kernel(qkv_ref, z_ref, ba_ref, cw_ref, gp_ref, nw_ref, o_ref,
                s_sc, tail_sc, q_sc, k_sc, v_sc, u_sc, wq_sc, kd_sc, qk_sc):
    @pl.when(pl.program_id(1) == 0)
    def _():
        s_sc[...] = jnp.zeros_like(s_sc)
        tail_sc[...] = jnp.zeros_like(tail_sc)

    dk = GDN_HEAD_DIM
    n_kh, n_vh = GDN_K_HEADS, GDN_V_HEADS
    rep = n_vh // n_kh
    c = GDN_CHUNK_ROWS
    t = qkv_ref.shape[0]
    n_chunks = t // c

    def conv_slab(j):
        cols = slice(j * dk, (j + 1) * dk)
        x = qkv_ref[:, cols].astype(F32)
        y = _silu(_causal_conv(x, tail_sc[:, cols], cw_ref[:, cols]))
        tail_sc[:, cols] = x[t - SUBLANES:]
        return y

    for kh in range(n_kh):
        q = conv_slab(kh)
        q_sc[kh] = q * lax.rsqrt(jnp.sum(q * q, axis=-1, keepdims=True) + NORM_EPS) * (dk ** -0.5)
        k = conv_slab(n_kh + kh)
        k_sc[kh] = k * lax.rsqrt(jnp.sum(k * k, axis=-1, keepdims=True) + NORM_EPS)
    for h in range(n_vh):
        v_sc[h] = conv_slab(2 * n_kh + h)

    ba = ba_ref[...]
    lane = lax.broadcasted_iota(jnp.int32, ba.shape, 1)
    gates = jnp.where(lane >= n_vh, -jnp.exp(gp_ref[0:1]) * _softplus(ba + gp_ref[1:2]),
                      jax.nn.sigmoid(ba))

    ri = lax.broadcasted_iota(jnp.int32, (c, c), 0)
    ci = lax.broadcasted_iota(jnp.int32, (c, c), 1)
    incl = (ci <= ri).astype(F32)
    incl_t = (ri <= ci).astype(F32)
    lower = ri >= ci
    strict = ri > ci

    chunk_decay = {}
    for n in range(n_chunks):
        rows = slice(n * c, (n + 1) * c)
        gch = gates[rows]
        gc = _mm_f32(incl, gch)
        gc_t = lax.dot_general(gch, incl_t, (((0,), (0,)), ((), ())), preferred_element_type=F32,
                               precision=lax.Precision.HIGHEST)
        for g0 in range(0, n_vh, GDN_HEAD_GROUP):
            heads = list(range(g0, g0 + GDN_HEAD_GROUP))
            gcol = {h: gc[:, n_vh + h:n_vh + h + 1] for h in heads}
            beta = {h: gch[:, h:h + 1] for h in heads}
            decay = {}
            for h in heads:
                diff = gcol[h] - gc_t[n_vh + h:n_vh + h + 1, :]
                decay[h] = jnp.where(lower, jnp.exp(jnp.where(lower, diff, 0.0)), 0.0)
                chunk_decay[h, n] = jnp.exp(gc[c - 1:c, n_vh + h:n_vh + h + 1])
            kb = {h: k_sc[h // rep, rows] * beta[h] for h in heads}
            kk = {h: _mm_nt(kb[h], k_sc[h // rep, rows]) for h in heads}
            qk = {kh: _mm_nt(q_sc[kh, rows], k_sc[kh, rows]) for kh in sorted({h // rep for h in heads})}
            t_inv = _unit_lower_inverse_batch([jnp.where(strict, kk[h] * decay[h], 0.0) for h in heads])
            egc = {h: jnp.exp(gcol[h]) for h in heads}
            uw = [_mm(ti, jnp.concatenate([v_sc[h, rows] * beta[h], kb[h] * egc[h]], axis=1))
                  for ti, h in zip(t_inv, heads)]
            for h, uw_h in zip(heads, uw):
                u_sc[h, rows] = uw_h[:, :dk]
                wq_sc[h, n, :c] = uw_h[:, dk:].astype(BF16)
                wq_sc[h, n, c:] = (q_sc[h // rep, rows] * egc[h]).astype(BF16)
                gl = gc[c - 1:c, n_vh + h:n_vh + h + 1]
                kd_sc[h, rows] = (k_sc[h // rep, rows] * jnp.exp(gl - gcol[h])).astype(BF16)
                qk_sc[h, rows] = (qk[h // rep] * decay[h]).astype(BF16)

    states = [s_sc[h] for h in range(n_vh)]
    for n in range(n_chunks):
        rows = slice(n * c, (n + 1) * c)
        ws_qs = [jnp.dot(wq_sc[h, n], states[h].astype(BF16), preferred_element_type=F32) for h in range(n_vh)]
        v_new = [(u_sc[h, rows] - ws_qs[h][:c]).astype(BF16) for h in range(n_vh)]
        outs = [ws_qs[h][c:] + jnp.dot(qk_sc[h, rows], v_new[h], preferred_element_type=F32)
                for h in range(n_vh)]
        states = [states[h] * chunk_decay[h, n]
                  + lax.dot_general(kd_sc[h, rows], v_new[h], (((0,), (0,)), ((), ())),
                                    preferred_element_type=F32) for h in range(n_vh)]
        for h in range(n_vh):
            cols = slice(h * dk, (h + 1) * dk)
            o = _rms(outs[h], nw_ref[...]) * _silu(z_ref[rows, cols].astype(F32))
            o_ref[rows, cols] = o.astype(o_ref.dtype)
    for h in range(n_vh):
        s_sc[h] = states[h]


def _gdn_core(qkv, z, ba, conv_w, a_log, dt_bias, norm_w, batch, seq):
    dk = GDN_HEAD_DIM
    n_kh, n_vh = GDN_K_HEADS, GDN_V_HEADS
    tt = min(GDN_TIME_TILE, seq)
    c = GDN_CHUNK_ROWS
    qkv = qkv.reshape(batch, seq, qkv.shape[-1])
    z = z.reshape(batch, seq, z.shape[-1])
    ba = ba.reshape(batch, seq, ba.shape[-1])
    pad = jnp.zeros((n_vh,), F32)
    gate_params = jnp.stack([jnp.concatenate([pad, a_log]), jnp.concatenate([pad, dt_bias])])
    blk = lambda n: pl.BlockSpec((None, tt, n), lambda b, t: (b, t, 0))
    out = pl.pallas_call(
        _gdn_kernel, grid=(batch, seq // tt),
        in_specs=[blk(qkv.shape[-1]), blk(z.shape[-1]), blk(ba.shape[-1]), _resident(conv_w.shape),
                  _resident(gate_params.shape), _resident((1, dk))],
        out_specs=blk(n_vh * dk),
        out_shape=jax.ShapeDtypeStruct((batch, seq, n_vh * dk), BF16),
        scratch_shapes=[pltpu.VMEM((n_vh, dk, dk), F32),
                        pltpu.VMEM((SUBLANES, qkv.shape[-1]), F32),
                        pltpu.VMEM((n_kh, tt, dk), F32),
                        pltpu.VMEM((n_kh, tt, dk), F32),
                        pltpu.VMEM((n_vh, tt, dk), F32),
                        pltpu.VMEM((n_vh, tt, dk), F32),
                        pltpu.VMEM((n_vh, tt // c, 2 * c, dk), BF16),
                        pltpu.VMEM((n_vh, tt, dk), BF16),
                        pltpu.VMEM((n_vh, tt, c), BF16)],
        compiler_params=_params("parallel", "arbitrary"), name="gdn_core",
    )(qkv, z, ba, conv_w, gate_params, norm_w.reshape(1, dk))
    return out.reshape(batch * seq, n_vh * dk)


def _rotate_half_columns(w, heads):
    k = w.shape[0]
    w = w.reshape(k, heads, 2, MLA_ROPE // 2)
    return jnp.concatenate([-w[:, :, 1:], w[:, :, :1]], axis=2).reshape(k, heads * MLA_ROPE)


def _hybrid_layer(x, cos, sin, npre, npost, w_in, conv_w, conv_b, gate_a_w, gate_a_b, gate_x_w, gate_x_b,
                  lam, q_norm, w_uq, kv_norm, w_ukv, w_out, batch, seq):
    rg_w = lam.shape[0]
    o1, o2, o3, o4 = rg_w, 2 * rg_w, 2 * rg_w + Q_LORA, 2 * rg_w + Q_LORA + KV_LORA
    w_kpe = w_in[:, o4:]
    w_ckv = jnp.concatenate([w_in[:, o3:o4], w_kpe, _rotate_half_columns(w_kpe, 1)], axis=1)
    weights = [w_in[:, :o1], w_in[:, o1:o2], w_in[:, o2:o3], w_ckv]
    x_r, gate_r, c_q, ckv = _norm_proj(x, npre, [w.astype(BF16) for w in weights], [BF16] * 4)

    y_a = _rglru(x_r, gate_r, conv_w, conv_b, gate_a_w.astype(BF16), gate_a_b, gate_x_w.astype(BF16),
                 gate_x_b, lam, batch, seq)

    wq = w_uq.reshape(Q_LORA, MLA_HEADS, MLA_QK)
    wqn = wq[:, :, :MLA_NOPE].reshape(Q_LORA, MLA_HEADS * MLA_NOPE)
    wqr = wq[:, :, MLA_NOPE:].reshape(Q_LORA, MLA_HEADS * MLA_ROPE)
    wkv = w_ukv.reshape(KV_LORA, MLA_HEADS, MLA_NOPE + MLA_V)
    wk = wkv[:, :, :MLA_NOPE].reshape(KV_LORA, MLA_HEADS * MLA_NOPE)
    wv = wkv[:, :, MLA_NOPE:].reshape(KV_LORA, MLA_HEADS * MLA_V)
    q_nope, q_rope, k_nope, v, k_rope = _mla_proj(
        c_q, ckv, cos, sin, q_norm, kv_norm, wqn.astype(BF16), wqr.astype(BF16),
        _rotate_half_columns(wqr, MLA_HEADS).astype(BF16), wk.astype(BF16), wv.astype(BF16))
    y_b = _attention(q_nope, q_rope, k_nope, k_rope, v, batch, seq)

    w_out = w_out.astype(BF16)
    return _out_proj([y_a, y_b], [w_out[:rg_w], w_out[rg_w:]], x, npost)


def _gdn_layer(x, npre, npost, w_in, conv_w, a_log, dt_bias, norm_w, w_out, batch, seq):
    conv_c = conv_w.shape[1]
    v_w = GDN_V_HEADS * GDN_HEAD_DIM
    weights = [w_in[:, :conv_c].astype(BF16), w_in[:, conv_c:conv_c + v_w].astype(BF16),
               w_in[:, conv_c + v_w:].astype(BF16)]
    qkv, z, ba = _norm_proj(x, npre, weights, [BF16, BF16, F32])
    y = _gdn_core(qkv, z, ba, conv_w, a_log, dt_bias, norm_w, batch, seq)
    return _out_proj([y], [w_out.astype(BF16)], x, npost)


def kernel(x, positions, norm_mix_pre, norm_mix_post, norm_ffn_pre, norm_ffn_post,
           hy_w_in, rg_conv_w, rg_conv_b, rg_gate_a_w, rg_gate_a_b, rg_gate_x_w,
           rg_gate_x_b, rg_lambda, mla_q_norm, mla_w_uq, mla_kv_norm, mla_w_ukv, hy_w_out,
           gdn_w_in, gdn_conv_w, gdn_a_log, gdn_dt_bias, gdn_norm, gdn_w_out,
           ffn_w_gate, ffn_w_up, ffn_w_down):
    batch, seq, d = x.shape
    depth = norm_mix_pre.shape[0]
    cos, sin = _rope_tables(positions)
    x = x.reshape(batch * seq, d)
    for layer in range(depth):
        i = layer // 2
        if layer % 2 == 0:
            x = _hybrid_layer(x, cos, sin, norm_mix_pre[layer], norm_mix_post[layer], hy_w_in[i],
                              rg_conv_w[i], rg_conv_b[i], rg_gate_a_w[i], rg_gate_a_b[i], rg_gate_x_w[i],
                              rg_gate_x_b[i], rg_lambda[i], mla_q_norm[i], mla_w_uq[i], mla_kv_norm[i],
                              mla_w_ukv[i], hy_w_out[i], batch, seq)
        else:
            x = _gdn_layer(x, norm_mix_pre[layer], norm_mix_post[layer], gdn_w_in[i], gdn_conv_w[i],
                           gdn_a_log[i], gdn_dt_bias[i], gdn_norm[i], gdn_w_out[i], batch, seq)
        x = _ffn(x, norm_ffn_pre[layer], ffn_w_gate[layer].astype(BF16), ffn_w_up[layer].astype(BF16),
                 ffn_w_down[layer].astype(BF16), norm_ffn_post[layer])
    return x.reshape(batch, seq, d)
```

```python
import functools

import jax
import jax.numpy as jnp
from jax import lax
from jax.experimental import pallas as pl
from jax.experimental.pallas import tpu as pltpu

F32 = jnp.float32
BF16 = jnp.bfloat16

NORM_EPS = 1e-6
LANES = 128
SUBLANES = 8
VMEM_LIMIT_BYTES = 48 * 1024 * 1024

CONV_WIDTH = 4
RG_BLOCKS = 8
RG_C = 8.0
MLA_HEADS = 8
MLA_NOPE = 128
MLA_ROPE = 64
MLA_V = 128
MLA_QK = MLA_NOPE + MLA_ROPE
Q_LORA = 512
KV_LORA = 256
ROPE_THETA = 10000.0
GDN_K_HEADS = 8
GDN_V_HEADS = 16
GDN_HEAD_DIM = 128
GDN_CHUNK_ROWS = 128
GDN_SOLVE_BLOCK = 16
GDN_HEAD_GROUP = 8

ROW_TILE = 512
ATTN_TILE = 256
RG_TIME_TILE = 256
GDN_TIME_TILE = 256


def _params(*semantics):
    return pltpu.CompilerParams(dimension_semantics=semantics, vmem_limit_bytes=VMEM_LIMIT_BYTES)


def _rms(x, w):
    return x * lax.rsqrt(jnp.mean(x * x, axis=-1, keepdims=True) + NORM_EPS) * w


def _mm(a, b):
    return jnp.dot(a.astype(BF16), b.astype(BF16), preferred_element_type=F32)


def _mm_f32(a, b):
    return jnp.dot(a, b, preferred_element_type=F32, precision=lax.Precision.HIGHEST)


def _mm_nt(a, b):
    return lax.dot_general(a.astype(BF16), b.astype(BF16), (((1,), (1,)), ((), ())),
                           preferred_element_type=F32)


def _mm_tn(a, b):
    return lax.dot_general(a.astype(BF16), b.astype(BF16), (((0,), (0,)), ((), ())),
                           preferred_element_type=F32)


def _resident(shape):
    zeros = (0,) * len(shape)
    return pl.BlockSpec(shape, lambda *_: zeros, pipeline_mode=pl.Buffered(1))


def _shift_matrices(t):
    r = lax.broadcasted_iota(jnp.int32, (t, t), 0)
    c = lax.broadcasted_iota(jnp.int32, (t, t), 1)
    return [(r - c == s).astype(BF16) for s in range(1, CONV_WIDTH)]


def _causal_conv(x_bf16, prev_tail, w, shifts):
    x = x_bf16.astype(F32)
    y = x * w[CONV_WIDTH - 1:CONV_WIDTH]
    row = lax.broadcasted_iota(jnp.int32, prev_tail.shape, 0)
    head_fix = jnp.zeros(prev_tail.shape, F32)
    for s, shift in zip(range(1, CONV_WIDTH), shifts):
        w_s = w[CONV_WIDTH - 1 - s:CONV_WIDTH - s]
        y = y + jnp.dot(shift, x_bf16, preferred_element_type=F32) * w_s
        head_fix = head_fix + jnp.where(row < s, pltpu.roll(prev_tail, s, axis=0), 0.0) * w_s
    return jnp.concatenate([y[:SUBLANES] + head_fix, y[SUBLANES:]], axis=0), x


def _causal_conv_vpu(x, prev_tail, w):
    t = x.shape[0]
    row = lax.broadcasted_iota(jnp.int32, prev_tail.shape, 0)
    y = x * w[CONV_WIDTH - 1:CONV_WIDTH]
    for s in range(1, CONV_WIDTH):
        xs = pltpu.roll(x, s, axis=0)
        first = jnp.where(row < s, pltpu.roll(prev_tail, s, axis=0), xs[:SUBLANES])
        y = y + jnp.concatenate([first, xs[SUBLANES:]], axis=0) * w[CONV_WIDTH - 1 - s:CONV_WIDTH - s]
    return y


def _sigmoid(x):
    return 0.5 + 0.5 * jnp.tanh(0.5 * x)


def _silu(x):
    h = 0.5 * x
    return h + h * jnp.tanh(h)


def _softplus(x):
    return jnp.maximum(x, 0.0) + jnp.log1p(jnp.exp(-jnp.abs(x)))


def _norm_proj_kernel(x_ref, nw_ref, *refs, n_out):
    w_refs, o_refs = refs[:n_out], refs[n_out:]
    h = _rms(x_ref[...], nw_ref[...]).astype(BF16)
    for w_ref, o_ref in zip(w_refs, o_refs):
        o_ref[...] = jnp.dot(h, w_ref[...], preferred_element_type=F32).astype(o_ref.dtype)


def _norm_proj(x, nw, weights, out_dtypes):
    m, d = x.shape
    tm = min(ROW_TILE, m)
    n_out = len(weights)
    in_specs = [pl.BlockSpec((tm, d), lambda i: (i, 0)), _resident((1, d))]
    in_specs += [_resident(w.shape) for w in weights]
    out_specs = [pl.BlockSpec((tm, w.shape[1]), lambda i: (i, 0)) for w in weights]
    out_shape = [jax.ShapeDtypeStruct((m, w.shape[1]), dt) for w, dt in zip(weights, out_dtypes)]
    return pl.pallas_call(
        functools.partial(_norm_proj_kernel, n_out=n_out),
        grid=(m // tm,), in_specs=in_specs, out_specs=out_specs, out_shape=out_shape,
        compiler_params=_params("parallel"), name="norm_proj",
    )(x, nw.reshape(1, d), *weights)


def _out_proj_kernel(*refs, n_in):
    a_refs, w_refs = refs[:n_in], refs[n_in:2 * n_in]
    x_ref, nw_ref, o_ref = refs[2 * n_in:]
    acc = jnp.dot(a_refs[0][...], w_refs[0][...], preferred_element_type=F32)
    for a_ref, w_ref in zip(a_refs[1:], w_refs[1:]):
        acc = acc + jnp.dot(a_ref[...], w_ref[...], preferred_element_type=F32)
    o_ref[...] = x_ref[...] + _rms(acc, nw_ref[...])


def _out_proj(acts, weights, x, nw):
    m, d = x.shape
    tm = min(ROW_TILE, m)
    n_in = len(acts)
    in_specs = [pl.BlockSpec((tm, a.shape[1]), lambda i: (i, 0)) for a in acts]
    in_specs += [_resident(w.shape) for w in weights]
    in_specs += [pl.BlockSpec((tm, d), lambda i: (i, 0)), _resident((1, d))]
    return pl.pallas_call(
        functools.partial(_out_proj_kernel, n_in=n_in),
        grid=(m // tm,), in_specs=in_specs,
        out_specs=pl.BlockSpec((tm, d), lambda i: (i, 0)),
        out_shape=jax.ShapeDtypeStruct((m, d), F32),
        input_output_aliases={2 * n_in: 0},
        compiler_params=_params("parallel"), name="out_proj",
    )(*acts, *weights, x, nw.reshape(1, d))


def _ffn_kernel(x_ref, npre_ref, wg_ref, wu_ref, wd_ref, npost_ref, o_ref, *, hidden_tile):
    x = x_ref[...]
    h = _rms(x, npre_ref[...]).astype(BF16)
    hidden = wg_ref.shape[1]
    acc = None
    for c0 in range(0, hidden, hidden_tile):
        g = jnp.dot(h, wg_ref[:, c0:c0 + hidden_tile], preferred_element_type=F32)
        u = jnp.dot(h, wu_ref[:, c0:c0 + hidden_tile], preferred_element_type=F32)
        a = (_silu(g) * u).astype(BF16)
        part = jnp.dot(a, wd_ref[c0:c0 + hidden_tile, :], preferred_element_type=F32)
        acc = part if acc is None else acc + part
    o_ref[...] = x + _rms(acc, npost_ref[...])


def _ffn(x, npre, wg, wu, wd, npost):
    m, d = x.shape
    tm = min(ROW_TILE, m)
    hidden = wg.shape[1]
    hidden_tile = hidden // 2 if (hidden // 2) % LANES == 0 else hidden
    return pl.pallas_call(
        functools.partial(_ffn_kernel, hidden_tile=hidden_tile),
        grid=(m // tm,),
        in_specs=[pl.BlockSpec((tm, d), lambda i: (i, 0)), _resident((1, d)),
                  _resident(wg.shape), _resident(wu.shape), _resident(wd.shape), _resident((1, d))],
        out_specs=pl.BlockSpec((tm, d), lambda i: (i, 0)),
        out_shape=jax.ShapeDtypeStruct((m, d), F32),
        input_output_aliases={0: 0},
        compiler_params=_params("parallel"), name="ffn",
    )(x, npre.reshape(1, d), wg, wu, wd, npost.reshape(1, d))


def _rope_table_kernel(pos_ref, invf_ref, cos_ref, sin_ref):
    ang = pos_ref[...].astype(F32) * invf_ref[...]
    cos_ref[...] = jnp.cos(ang)
    sin_ref[...] = jnp.sin(ang)


def _rope_tables(positions):
    m = positions.size
    tm = min(ROW_TILE, m)
    half = MLA_ROPE // 2
    inv_freq = 1.0 / (ROPE_THETA ** (jnp.arange(0, MLA_ROPE, 2, dtype=F32) / MLA_ROPE))
    invf = jnp.tile(inv_freq, LANES // half).reshape(1, LANES)
    return pl.pallas_call(
        _rope_table_kernel, grid=(m // tm,),
        in_specs=[pl.BlockSpec((tm, 1), lambda i: (i, 0)), _resident((1, LANES))],
        out_specs=[pl.BlockSpec((tm, LANES), lambda i: (i, 0))] * 2,
        out_shape=[jax.ShapeDtypeStruct((m, LANES), F32)] * 2,
        compiler_params=_params("parallel"), name="rope_tables",
    )(positions.reshape(m, 1), invf)


def _rglru_kernel(xr_ref, gr_ref, cw_ref, cb_ref, wa_ref, ba_ref, wx_ref, bx_ref, lam_ref, o_ref,
                  h_sc, tail_sc):
    @pl.when(pl.program_id(1) == 0)
    def _():
        h_sc[...] = jnp.zeros_like(h_sc)
        tail_sc[...] = jnp.zeros_like(tail_sc)

    t, width = xr_ref.shape
    xc, x = _causal_conv(xr_ref[...], tail_sc[...], cw_ref[...], _shift_matrices(t))
    xc = xc + cb_ref[...]
    tail_sc[...] = x[t - SUBLANES:]

    xcb = xc.astype(BF16)
    bw = width // RG_BLOCKS
    ra = jnp.concatenate([jnp.dot(xcb[:, n * bw:(n + 1) * bw], wa_ref[n], preferred_element_type=F32)
                          for n in range(RG_BLOCKS)], axis=1)
    rx = jnp.concatenate([jnp.dot(xcb[:, n * bw:(n + 1) * bw], wx_ref[n], preferred_element_type=F32)
                          for n in range(RG_BLOCKS)], axis=1)
    r = _sigmoid(ra + ba_ref[...])
    i = _sigmoid(rx + bx_ref[...])
    log_a = (-RG_C * _softplus(-lam_ref[...])) * r
    a = jnp.exp(log_a)
    th = jnp.tanh(log_a)
    b = jnp.sqrt(-2.0 * th / (1.0 - th)) * (i * xc)

    groups = t // SUBLANES
    a = a.reshape(groups, SUBLANES, width)
    b = b.reshape(groups, SUBLANES, width)
    sub = lax.broadcasted_iota(jnp.int32, (groups, SUBLANES, width), 1)
    s = 1
    while s < SUBLANES:
        keep = sub >= s
        b = jnp.where(keep, a * pltpu.roll(b, s, axis=1) + b, b)
        a = jnp.where(keep, a * pltpu.roll(a, s, axis=1), a)
        s *= 2
    carry = h_sc[0:1]
    hs = []
    for g in range(groups):
        hs.append(b[g] + a[g] * carry)
        carry = hs[-1][SUBLANES - 1:]
    h_sc[...] = jnp.broadcast_to(carry, h_sc.shape)
    h = jnp.concatenate(hs, axis=0)
    o_ref[...] = (h * jax.nn.gelu(gr_ref[...].astype(F32))).astype(o_ref.dtype)


def _rglru(x_r, gate_r, conv_w, conv_b, wa, ba, wx, bx, lam, batch, seq):
    width = x_r.shape[-1]
    tt = min(RG_TIME_TILE, seq)
    x_r = x_r.reshape(batch, seq, width)
    gate_r = gate_r.reshape(batch, seq, width)
    blk = pl.BlockSpec((None, tt, width), lambda b, t: (b, t, 0))
    vec = _resident((1, width))
    out = pl.pallas_call(
        _rglru_kernel, grid=(batch, seq // tt),
        in_specs=[blk, blk, _resident(conv_w.shape), vec, _resident(wa.shape), vec,
                  _resident(wx.shape), vec, vec],
        out_specs=blk,
        out_shape=jax.ShapeDtypeStruct((batch, seq, width), BF16),
        scratch_shapes=[pltpu.VMEM((SUBLANES, width), F32), pltpu.VMEM((SUBLANES, width), F32)],
        compiler_params=_params("parallel", "arbitrary"), name="rglru",
    )(x_r, gate_r, conv_w, conv_b.reshape(1, width), wa, ba.reshape(1, width), wx,
      bx.reshape(1, width), lam.reshape(1, width))
    return out.reshape(batch * seq, width)


ATTN_HEAD_WIDTH = 2 * LANES
LOG2_E = 1.4426950408889634


def _mla_proj_kernel(cq_ref, ckv_ref, cos_ref, sin_ref, qnw_ref, kvnw_ref, wq_ref, wk_ref, wv_ref,
                     q_o, k_o, v_o):
    tm = cq_ref.shape[0]
    hw = ATTN_HEAD_WIDTH
    half = MLA_ROPE // 2
    lane = lax.broadcasted_iota(jnp.int32, (tm, LANES), 1)
    cos = jnp.where(lane < MLA_ROPE, cos_ref[...], 0.0)
    sin = jnp.where(lane < MLA_ROPE, sin_ref[...], 0.0)

    def rope(x):
        rot = jnp.where(lane < half, -pltpu.roll(x, LANES - half, axis=1), pltpu.roll(x, half, axis=1))
        return x * cos + rot * sin

    q_scale = MLA_QK ** -0.5 * LOG2_E
    qn = _rms(cq_ref[...].astype(F32), qnw_ref[...]).astype(BF16)
    q = jnp.dot(qn, wq_ref[...], preferred_element_type=F32) * q_scale
    ckv = ckv_ref[...]
    kvn = _rms(ckv[:, :KV_LORA].astype(F32), kvnw_ref[...]).astype(BF16)
    k_nope = jnp.dot(kvn, wk_ref[...], preferred_element_type=F32)
    v = jnp.dot(kvn, wv_ref[...], preferred_element_type=F32)
    k_rope = rope(ckv[:, KV_LORA:].astype(F32)).astype(k_o.dtype)
    ones = jnp.ones((tm, LANES), v_o.dtype)
    for h in range(MLA_HEADS):
        lo, mid, hi = h * hw, h * hw + LANES, (h + 1) * hw
        q_o[:, lo:mid] = q[:, lo:mid].astype(q_o.dtype)
        q_o[:, mid:hi] = rope(q[:, mid:hi]).astype(q_o.dtype)
        k_o[:, lo:mid] = k_nope[:, h * MLA_NOPE:(h + 1) * MLA_NOPE].astype(k_o.dtype)
        k_o[:, mid:hi] = k_rope
        v_o[:, lo:mid] = v[:, h * MLA_V:(h + 1) * MLA_V].astype(v_o.dtype)
        v_o[:, mid:hi] = ones


def _mla_proj(c_q, ckv, cos, sin, q_norm, kv_norm, wq, wk, wv):
    m = c_q.shape[0]
    tm = min(ROW_TILE, m)
    row = lambda n: pl.BlockSpec((tm, n), lambda i: (i, 0))
    width = MLA_HEADS * ATTN_HEAD_WIDTH
    return pl.pallas_call(
        _mla_proj_kernel, grid=(m // tm,),
        in_specs=[row(c_q.shape[1]), row(ckv.shape[1]), row(LANES), row(LANES),
                  _resident((1, Q_LORA)), _resident((1, KV_LORA)), _resident(wq.shape),
                  _resident(wk.shape), _resident(wv.shape)],
        out_specs=[row(width)] * 3,
        out_shape=[jax.ShapeDtypeStruct((m, width), BF16)] * 3,
        compiler_params=_params("parallel"), name="mla_proj",
    )(c_q, ckv, cos, sin, q_norm.reshape(1, Q_LORA), kv_norm.reshape(1, KV_LORA), wq, wk, wv)


def _attn_kernel(q_ref, k_ref, v_ref, o_ref, acc_sc, *, tile):
    qi = pl.program_id(1)
    hw = ATTN_HEAD_WIDTH
    heads = range(MLA_HEADS)
    row = lax.broadcasted_iota(jnp.int32, (tile, tile), 0)
    col = lax.broadcasted_iota(jnp.int32, (tile, tile), 1)
    acc_sc[...] = jnp.zeros_like(acc_sc)

    def block(j, m_prev, masked):
        rows = pl.ds(pl.multiple_of(j * tile, tile), tile)
        s = [lax.dot_general(q_ref[:, h * hw:(h + 1) * hw], k_ref[rows, h * hw:(h + 1) * hw],
                             (((1,), (1,)), ((), ())), preferred_element_type=F32) for h in heads]
        if masked:
            s = [jnp.where(row >= col, x, -jnp.inf) for x in s]
        m_new = [jnp.maximum(m, jnp.max(x, axis=-1, keepdims=True)) for m, x in zip(m_prev, s)]
        alpha = [jnp.exp2(m - mn) for m, mn in zip(m_prev, m_new)]
        p = [jnp.exp2(x - mn).astype(BF16) for x, mn in zip(s, m_new)]
        pv = [jnp.dot(p[h], v_ref[rows, h * hw:(h + 1) * hw], preferred_element_type=F32) for h in heads]
        for h in heads:
            acc_sc[h] = alpha[h] * acc_sc[h] + pv[h]
        return m_new

    init = [jnp.full((tile, 1), -jnp.inf, F32) for _ in heads]
    m_run = lax.fori_loop(0, qi, lambda j, m: block(j, m, False), init)
    block(qi, m_run, True)
    for h in heads:
        acc = acc_sc[h]
        o_ref[:, h * MLA_V:(h + 1) * MLA_V] = (acc[:, :MLA_V] / acc[:, LANES:LANES + MLA_V]).astype(o_ref.dtype)


def _attention(q, k, v, batch, seq):
    tile = min(ATTN_TILE, seq)
    width = q.shape[-1]
    r3 = lambda a: a.reshape(batch, seq, width)
    kv_blk = pl.BlockSpec((None, seq, width), lambda b, t: (b, 0, 0), pipeline_mode=pl.Buffered(1))
    out = pl.pallas_call(
        functools.partial(_attn_kernel, tile=tile),
        grid=(batch, seq // tile),
        in_specs=[pl.BlockSpec((None, tile, width), lambda b, t: (b, t, 0)), kv_blk, kv_blk],
        out_specs=pl.BlockSpec((None, tile, MLA_HEADS * MLA_V), lambda b, t: (b, t, 0)),
        out_shape=jax.ShapeDtypeStruct((batch, seq, MLA_HEADS * MLA_V), BF16),
        scratch_shapes=[pltpu.VMEM((MLA_HEADS, tile, ATTN_HEAD_WIDTH), F32)],
        compiler_params=_params("parallel", "arbitrary"), name="mla_attention",
    )(r3(q), r3(k), r3(v))
    return out.reshape(batch * seq, MLA_HEADS * MLA_V)


def _nilpotent_inverse_batch(mats, order, eye):
    xs = [eye - m for m in mats]
    ps = mats
    k = 1
    while 2 * k < order:
        ps = [_mm(p, p) for p in ps]
        xs = [x + _mm(x, p) for x, p in zip(xs, ps)]
        k *= 2
    return xs


def _unit_lower_inverse_batch(mats):
    c = mats[0].shape[0]
    r = lax.broadcasted_iota(jnp.int32, (c, c), 0)
    q = lax.broadcasted_iota(jnp.int32, (c, c), 1)
    eye = (r == q).astype(F32)
    same_block = (r // GDN_SOLVE_BLOCK) == (q // GDN_SOLVE_BLOCK)
    ds = [jnp.where(same_block, a, 0.0) for a in mats]
    lows = [a - d for a, d in zip(mats, ds)]
    xs = _nilpotent_inverse_batch(ds, GDN_SOLVE_BLOCK, eye)
    ns = [_mm(x, low) for x, low in zip(xs, lows)]
    ys = _nilpotent_inverse_batch(ns, c // GDN_SOLVE_BLOCK, eye)
    return [_mm(y, x) for y, x in zip(ys, xs)]


def _gdn_kernel(qkv_ref, z_ref, ba_ref, cw_ref, gp_ref, nw_ref, o_ref,
                s_sc, tail_sc, q_sc, k_sc, v_sc, u_sc, wq_sc, kd_sc, qk_sc):
    @pl.when(pl.program_id(1) == 0)
    def _():
        s_sc[...] = jnp.zeros_like(s_sc)
        tail_sc[...] = jnp.zeros_like(tail_sc)

    dk = GDN_HEAD_DIM
    n_kh, n_vh = GDN_K_HEADS, GDN_V_HEADS
    rep = n_vh // n_kh
    c = GDN_CHUNK_ROWS
    t = qkv_ref.shape[0]
    n_chunks = t // c

    def conv_slab(j):
        cols = slice(j * dk, (j + 1) * dk)
        x = qkv_ref[:, cols].astype(F32)
        y = _silu(_causal_conv_vpu(x, tail_sc[:, cols], cw_ref[:, cols]))
        tail_sc[:, cols] = x[t - SUBLANES:]
        return y

    def l2_normalized(x, scale):
        return x * (lax.rsqrt(jnp.sum(x * x, axis=-1, keepdims=True) + NORM_EPS) * scale)

    for kh in range(n_kh):
        q_sc[kh] = l2_normalized(conv_slab(kh), dk ** -0.5)
        k_sc[kh] = l2_normalized(conv_slab(n_kh + kh), 1.0)
    for h in range(n_vh):
        v_sc[h] = conv_slab(2 * n_kh + h)

    ba = ba_ref[...]
    lane = lax.broadcasted_iota(jnp.int32, ba.shape, 1)
    gates = jnp.where(lane >= n_vh, -jnp.exp(gp_ref[0:1]) * _softplus(ba + gp_ref[1:2]),
                      _sigmoid(ba))

    ri = lax.broadcasted_iota(jnp.int32, (c, c), 0)
    ci = lax.broadcasted_iota(jnp.int32, (c, c), 1)
    incl = (ci <= ri).astype(F32)
    incl_t = (ri <= ci).astype(F32)
    lower = ri >= ci
    strict = ri > ci

    chunk_decay = {}
    for n in range(n_chunks):
        rows = slice(n * c, (n + 1) * c)
        gch = gates[rows]
        gc = _mm_f32(incl, gch)
        gc_t = lax.dot_general(gch, incl_t, (((0,), (0,)), ((), ())), preferred_element_type=F32,
                               precision=lax.Precision.HIGHEST)
        for g0 in range(0, n_vh, GDN_HEAD_GROUP):
            heads = list(range(g0, g0 + GDN_HEAD_GROUP))
            gcol = {h: gc[:, n_vh + h:n_vh + h + 1] for h in heads}
            beta = {h: gch[:, h:h + 1] for h in heads}
            decay = {}
            for h in heads:
                diff = gcol[h] - gc_t[n_vh + h:n_vh + h + 1, :]
                decay[h] = jnp.where(lower, jnp.exp(jnp.where(lower, diff, 0.0)), 0.0)
                chunk_decay[h, n] = jnp.exp(gc[c - 1:c, n_vh + h:n_vh + h + 1])
            kb = {h: k_sc[h // rep, rows] * beta[h] for h in heads}
            kk = {h: _mm_nt(kb[h], k_sc[h // rep, rows]) for h in heads}
            qk = {kh: _mm_nt(q_sc[kh, rows], k_sc[kh, rows]) for kh in sorted({h // rep for h in heads})}
            t_inv = _unit_lower_inverse_batch([jnp.where(strict, kk[h] * decay[h], 0.0) for h in heads])
            egc = {h: jnp.exp(gcol[h]) for h in heads}
            uw = [_mm(ti, jnp.concatenate([v_sc[h, rows] * beta[h], kb[h] * egc[h]], axis=1))
                  for ti, h in zip(t_inv, heads)]
            for h, uw_h in zip(heads, uw):
                u_sc[h, rows] = uw_h[:, :dk]
                wq_sc[h, n, :c] = uw_h[:, dk:].astype(BF16)
                wq_sc[h, n, c:] = (q_sc[h // rep, rows] * egc[h]).astype(BF16)
                gl = gc[c - 1:c, n_vh + h:n_vh + h + 1]
                kd_sc[h, rows] = (k_sc[h // rep, rows] * jnp.exp(gl - gcol[h])).astype(BF16)
                qk_sc[h, rows] = (qk[h // rep] * decay[h]).astype(BF16)

    states = [s_sc[h] for h in range(n_vh)]
    for n in range(n_chunks):
        rows = slice(n * c, (n + 1) * c)
        ws_qs = [jnp.dot(wq_sc[h, n], states[h].astype(BF16), preferred_element_type=F32) for h in range(n_vh)]
        v_new = [(u_sc[h, rows] - ws_qs[h][:c]).astype(BF16) for h in range(n_vh)]
        outs = [ws_qs[h][c:] + jnp.dot(qk_sc[h, rows], v_new[h], preferred_element_type=F32)
                for h in range(n_vh)]
        states = [states[h] * chunk_decay[h, n]
                  + lax.dot_general(kd_sc[h, rows], v_new[h], (((0,), (0,)), ((), ())),
                                    preferred_element_type=F32) for h in range(n_vh)]
        for h in range(n_vh):
            cols = slice(h * dk, (h + 1) * dk)
            o = _rms(outs[h], nw_ref[...]) * _silu(z_ref[rows, cols].astype(F32))
            o_ref[rows, cols] = o.astype(o_ref.dtype)
    for h in range(n_vh):
        s_sc[h] = states[h]


def _gdn_core(qkv, z, ba, conv_w, a_log, dt_bias, norm_w, batch, seq):
    dk = GDN_HEAD_DIM
    n_kh, n_vh = GDN_K_HEADS, GDN_V_HEADS
    tt = min(GDN_TIME_TILE, seq)
    c = GDN_CHUNK_ROWS
    qkv = qkv.reshape(batch, seq, qkv.shape[-1])
    z = z.reshape(batch, seq, z.shape[-1])
    ba = ba.reshape(batch, seq, ba.shape[-1])
    pad = jnp.zeros((n_vh,), F32)
    gate_params = jnp.stack([jnp.concatenate([pad, a_log]), jnp.concatenate([pad, dt_bias])])
    blk = lambda n: pl.BlockSpec((None, tt, n), lambda b, t: (b, t, 0))
    out = pl.pallas_call(
        _gdn_kernel, grid=(batch, seq // tt),
        in_specs=[blk(qkv.shape[-1]), blk(z.shape[-1]), blk(ba.shape[-1]), _resident(conv_w.shape),
                  _resident(gate_params.shape), _resident((1, dk))],
        out_specs=blk(n_vh * dk),
        out_shape=jax.ShapeDtypeStruct((batch, seq, n_vh * dk), BF16),
        scratch_shapes=[pltpu.VMEM((n_vh, dk, dk), F32),
                        pltpu.VMEM((SUBLANES, qkv.shape[-1]), F32),
                        pltpu.VMEM((n_kh, tt, dk), F32),
                        pltpu.VMEM((n_kh, tt, dk), F32),
                        pltpu.VMEM((n_vh, tt, dk), F32),
                        pltpu.VMEM((n_vh, tt, dk), F32),
                        pltpu.VMEM((n_vh, tt // c, 2 * c, dk), BF16),
                        pltpu.VMEM((n_vh, tt, dk), BF16),
                        pltpu.VMEM((n_vh, tt, c), BF16)],
        compiler_params=_params("parallel", "arbitrary"), name="gdn_core",
    )(qkv, z, ba, conv_w, gate_params, norm_w.reshape(1, dk))
    return out.reshape(batch * seq, n_vh * dk)


def _hybrid_layer(x, cos, sin, npre, npost, w_in, conv_w, conv_b, gate_a_w, gate_a_b, gate_x_w, gate_x_b,
                  lam, q_norm, w_uq, kv_norm, w_ukv, w_out, batch, seq):
    rg_w = lam.shape[0]
    o1, o2, o3, o4 = rg_w, 2 * rg_w, 2 * rg_w + Q_LORA, 2 * rg_w + Q_LORA + KV_LORA
    d_model = w_in.shape[0]
    w_ckv = jnp.concatenate([w_in[:, o3:], jnp.zeros((d_model, LANES - MLA_ROPE), w_in.dtype)], axis=1)
    weights = [w_in[:, :o1], w_in[:, o1:o2], w_in[:, o2:o3], w_ckv]
    x_r, gate_r, c_q, ckv = _norm_proj(x, npre, [w.astype(BF16) for w in weights], [BF16] * 4)

    y_a = _rglru(x_r, gate_r, conv_w, conv_b, gate_a_w.astype(BF16), gate_a_b, gate_x_w.astype(BF16),
                 gate_x_b, lam, batch, seq)

    wq = w_uq.reshape(Q_LORA, MLA_HEADS, MLA_QK)
    wq = jnp.concatenate([wq, jnp.zeros((Q_LORA, MLA_HEADS, ATTN_HEAD_WIDTH - MLA_QK), wq.dtype)], axis=2)
    wq = wq.reshape(Q_LORA, MLA_HEADS * ATTN_HEAD_WIDTH)
    wkv = w_ukv.reshape(KV_LORA, MLA_HEADS, MLA_NOPE + MLA_V)
    wk = wkv[:, :, :MLA_NOPE].reshape(KV_LORA, MLA_HEADS * MLA_NOPE)
    wv = wkv[:, :, MLA_NOPE:].reshape(KV_LORA, MLA_HEADS * MLA_V)
    q, k, v = _mla_proj(c_q, ckv, cos, sin, q_norm, kv_norm, wq.astype(BF16), wk.astype(BF16),
                        wv.astype(BF16))
    y_b = _attention(q, k, v, batch, seq)

    w_out = w_out.astype(BF16)
    return _out_proj([y_a, y_b], [w_out[:rg_w], w_out[rg_w:]], x, npost)


def _gdn_layer(x, npre, npost, w_in, conv_w, a_log, dt_bias, norm_w, w_out, batch, seq):
    conv_c = conv_w.shape[1]
    v_w = GDN_V_HEADS * GDN_HEAD_DIM
    weights = [w_in[:, :conv_c].astype(BF16), w_in[:, conv_c:conv_c + v_w].astype(BF16),
               w_in[:, conv_c + v_w:].astype(BF16)]
    qkv, z, ba = _norm_proj(x, npre, weights, [BF16, BF16, F32])
    y = _gdn_core(qkv, z, ba, conv_w, a_log, dt_bias, norm_w, batch, seq)
    return _out_proj([y], [w_out.astype(BF16)], x, npost)


def kernel(x, positions, norm_mix_pre, norm_mix_post, norm_ffn_pre, norm_ffn_post,
           hy_w_in, rg_conv_w, rg_conv_b, rg_gate_a_w, rg_gate_a_b, rg_gate_x_w,
           rg_gate_x_b, rg_lambda, mla_q_norm, mla_w_uq, mla_kv_norm, mla_w_ukv, hy_w_out,
           gdn_w_in, gdn_conv_w, gdn_a_log, gdn_dt_bias, gdn_norm, gdn_w_out,
           ffn_w_gate, ffn_w_up, ffn_w_down):
    batch, seq, d = x.shape
    depth = norm_mix_pre.shape[0]
    cos, sin = _rope_tables(positions)
    x = x.reshape(batch * seq, d)
    for layer in range(depth):
        i = layer // 2
        if layer % 2 == 0:
            x = _hybrid_layer(x, cos, sin, norm_mix_pre[layer], norm_mix_post[layer], hy_w_in[i],
                              rg_conv_w[i], rg_conv_b[i], rg_gate_a_w[i], rg_gate_a_b[i], rg_gate_x_w[i],
                              rg_gate_x_b[i], rg_lambda[i], mla_q_norm[i], mla_w_uq[i], mla_kv_norm[i],
                              mla_w_ukv[i], hy_w_out[i], batch, seq)
        else:
            x = _gdn_layer(x, norm_mix_pre[layer], norm_mix_post[layer], gdn_w_in[i], gdn_conv_w[i],
                           gdn_a_log[i], gdn_dt_bias[i], gdn_norm[i], gdn_w_out[i], batch, seq)
        x = _ffn(x, norm_ffn_pre[layer], ffn_w_gate[layer].astype(BF16), ffn_w_up[layer].astype(BF16),
                 ffn_w_down[layer].astype(BF16), norm_ffn_post[layer])
    return x.reshape(batch, seq, d)
```

```python
import functools

import jax
import jax.numpy as jnp
from jax import lax
from jax.experimental import pallas as pl
from jax.experimental.pallas import tpu as pltpu

F32 = jnp.float32
BF16 = jnp.bfloat16

NORM_EPS = 1e-6
LANES = 128
SUBLANES = 8
VMEM_LIMIT_BYTES = 48 * 1024 * 1024

CONV_WIDTH = 4
RG_BLOCKS = 8
RG_C = 8.0
MLA_HEADS = 8
MLA_NOPE = 128
MLA_ROPE = 64
MLA_V = 128
MLA_QK = MLA_NOPE + MLA_ROPE
Q_LORA = 512
KV_LORA = 256
ROPE_THETA = 10000.0
GDN_K_HEADS = 8
GDN_V_HEADS = 16
GDN_HEAD_DIM = 128
GDN_CHUNK_ROWS = 128
GDN_SOLVE_BLOCK = 16
GDN_PROJ_SUBTILE = 128
GDN_SYSTEM_BATCH = 16

ROW_TILE = 512
ATTN_TILE = 256
RG_TIME_TILE = 256
GDN_TIME_TILE = 256


def _params(*semantics):
    return pltpu.CompilerParams(dimension_semantics=semantics, vmem_limit_bytes=VMEM_LIMIT_BYTES)


def _rms(x, w):
    return x * lax.rsqrt(jnp.mean(x * x, axis=-1, keepdims=True) + NORM_EPS) * w


def _mm(a, b):
    return jnp.dot(a.astype(BF16), b.astype(BF16), preferred_element_type=F32)


def _mm_f32(a, b):
    return jnp.dot(a, b, preferred_element_type=F32, precision=lax.Precision.HIGHEST)


def _mm_nt(a, b):
    return lax.dot_general(a.astype(BF16), b.astype(BF16), (((1,), (1,)), ((), ())),
                           preferred_element_type=F32)


def _mm_tn(a, b):
    return lax.dot_general(a.astype(BF16), b.astype(BF16), (((0,), (0,)), ((), ())),
                           preferred_element_type=F32)


def _resident(shape):
    zeros = (0,) * len(shape)
    return pl.BlockSpec(shape, lambda *_: zeros, pipeline_mode=pl.Buffered(1))


def _shift_matrices(t):
    r = lax.broadcasted_iota(jnp.int32, (t, t), 0)
    c = lax.broadcasted_iota(jnp.int32, (t, t), 1)
    return [(r - c == s).astype(BF16) for s in range(1, CONV_WIDTH)]


def _causal_conv(x_bf16, prev_tail, w, shifts):
    x = x_bf16.astype(F32)
    y = x * w[CONV_WIDTH - 1:CONV_WIDTH]
    row = lax.broadcasted_iota(jnp.int32, prev_tail.shape, 0)
    head_fix = jnp.zeros(prev_tail.shape, F32)
    for s, shift in zip(range(1, CONV_WIDTH), shifts):
        w_s = w[CONV_WIDTH - 1 - s:CONV_WIDTH - s]
        y = y + jnp.dot(shift, x_bf16, preferred_element_type=F32) * w_s
        head_fix = head_fix + jnp.where(row < s, pltpu.roll(prev_tail, s, axis=0), 0.0) * w_s
    return jnp.concatenate([y[:SUBLANES] + head_fix, y[SUBLANES:]], axis=0), x


def _causal_conv_vpu(x, prev_tail, w):
    t = x.shape[0]
    row = lax.broadcasted_iota(jnp.int32, prev_tail.shape, 0)
    y = x * w[CONV_WIDTH - 1:CONV_WIDTH]
    for s in range(1, CONV_WIDTH):
        xs = pltpu.roll(x, s, axis=0)
        first = jnp.where(row < s, pltpu.roll(prev_tail, s, axis=0), xs[:SUBLANES])
        y = y + jnp.concatenate([first, xs[SUBLANES:]], axis=0) * w[CONV_WIDTH - 1 - s:CONV_WIDTH - s]
    return y


def _sigmoid(x):
    return 0.5 + 0.5 * jnp.tanh(0.5 * x)


def _silu(x):
    h = 0.5 * x
    return h + h * jnp.tanh(h)


def _softplus(x):
    return jnp.maximum(x, 0.0) + jnp.log1p(jnp.exp(-jnp.abs(x)))


def _norm_proj_kernel(x_ref, nw_ref, *refs, n_out):
    w_refs, o_refs = refs[:n_out], refs[n_out:]
    h = _rms(x_ref[...], nw_ref[...]).astype(BF16)
    for w_ref, o_ref in zip(w_refs, o_refs):
        o_ref[...] = jnp.dot(h, w_ref[...], preferred_element_type=F32).astype(o_ref.dtype)


def _norm_proj(x, nw, weights, out_dtypes):
    m, d = x.shape
    tm = min(ROW_TILE, m)
    n_out = len(weights)
    in_specs = [pl.BlockSpec((tm, d), lambda i: (i, 0)), _resident((1, d))]
    in_specs += [_resident(w.shape) for w in weights]
    out_specs = [pl.BlockSpec((tm, w.shape[1]), lambda i: (i, 0)) for w in weights]
    out_shape = [jax.ShapeDtypeStruct((m, w.shape[1]), dt) for w, dt in zip(weights, out_dtypes)]
    return pl.pallas_call(
        functools.partial(_norm_proj_kernel, n_out=n_out),
        grid=(m // tm,), in_specs=in_specs, out_specs=out_specs, out_shape=out_shape,
        compiler_params=_params("parallel"), name="norm_proj",
    )(x, nw.reshape(1, d), *weights)


def _out_proj_kernel(*refs, n_in):
    a_refs, w_refs = refs[:n_in], refs[n_in:2 * n_in]
    x_ref, nw_ref, o_ref = refs[2 * n_in:]
    acc = jnp.dot(a_refs[0][...], w_refs[0][...], preferred_element_type=F32)
    for a_ref, w_ref in zip(a_refs[1:], w_refs[1:]):
        acc = acc + jnp.dot(a_ref[...], w_ref[...], preferred_element_type=F32)
    o_ref[...] = x_ref[...] + _rms(acc, nw_ref[...])


def _out_proj(acts, weights, x, nw, in_place=True):
    m, d = x.shape
    tm = min(ROW_TILE, m)
    n_in = len(acts)
    in_specs = [pl.BlockSpec((tm, a.shape[1]), lambda i: (i, 0)) for a in acts]
    in_specs += [_resident(w.shape) for w in weights]
    in_specs += [pl.BlockSpec((tm, d), lambda i: (i, 0)), _resident((1, d))]
    return pl.pallas_call(
        functools.partial(_out_proj_kernel, n_in=n_in),
        grid=(m // tm,), in_specs=in_specs,
        out_specs=pl.BlockSpec((tm, d), lambda i: (i, 0)),
        out_shape=jax.ShapeDtypeStruct((m, d), F32),
        input_output_aliases={2 * n_in: 0} if in_place else {},
        compiler_params=_params("parallel"), name="out_proj",
    )(*acts, *weights, x, nw.reshape(1, d))


def _ffn_kernel(x_ref, npre_ref, wg_ref, wu_ref, wd_ref, npost_ref, o_ref, *, hidden_tile):
    x = x_ref[...]
    h = _rms(x, npre_ref[...]).astype(BF16)
    hidden = wg_ref.shape[1]
    acc = None
    for c0 in range(0, hidden, hidden_tile):
        g = jnp.dot(h, wg_ref[:, c0:c0 + hidden_tile], preferred_element_type=F32)
        u = jnp.dot(h, wu_ref[:, c0:c0 + hidden_tile], preferred_element_type=F32)
        a = (_silu(g) * u).astype(BF16)
        part = jnp.dot(a, wd_ref[c0:c0 + hidden_tile, :], preferred_element_type=F32)
        acc = part if acc is None else acc + part
    o_ref[...] = x + _rms(acc, npost_ref[...])


def _layer_slice(stacked, layer):
    shape = stacked.shape[1:]
    zeros = (0,) * len(shape)
    return pl.BlockSpec((None,) + shape, lambda *_: (layer,) + zeros, pipeline_mode=pl.Buffered(1))


def _ffn(x, npre, wg_all, wu_all, wd_all, npost, layer):
    m, d = x.shape
    tm = min(ROW_TILE, m)
    hidden = wg_all.shape[2]
    hidden_tile = hidden // 2 if (hidden // 2) % LANES == 0 else hidden
    return pl.pallas_call(
        functools.partial(_ffn_kernel, hidden_tile=hidden_tile),
        grid=(m // tm,),
        in_specs=[pl.BlockSpec((tm, d), lambda i: (i, 0)), _resident((1, d)),
                  _layer_slice(wg_all, layer), _layer_slice(wu_all, layer), _layer_slice(wd_all, layer),
                  _resident((1, d))],
        out_specs=pl.BlockSpec((tm, d), lambda i: (i, 0)),
        out_shape=jax.ShapeDtypeStruct((m, d), F32),
        input_output_aliases={0: 0},
        compiler_params=_params("parallel"), name="ffn",
    )(x, npre.reshape(1, d), wg_all, wu_all, wd_all, npost.reshape(1, d))


def _rope_table_kernel(pos_ref, invf_ref, cos_ref, sin_ref):
    ang = pos_ref[...].astype(F32) * invf_ref[...]
    cos_ref[...] = jnp.cos(ang)
    sin_ref[...] = jnp.sin(ang)


def _rope_tables(positions):
    m = positions.size
    tm = min(ROW_TILE, m)
    half = MLA_ROPE // 2
    inv_freq = 1.0 / (ROPE_THETA ** (jnp.arange(0, MLA_ROPE, 2, dtype=F32) / MLA_ROPE))
    invf = jnp.tile(inv_freq, LANES // half).reshape(1, LANES)
    return pl.pallas_call(
        _rope_table_kernel, grid=(m // tm,),
        in_specs=[pl.BlockSpec((tm, 1), lambda i: (i, 0)), _resident((1, LANES))],
        out_specs=[pl.BlockSpec((tm, LANES), lambda i: (i, 0))] * 2,
        out_shape=[jax.ShapeDtypeStruct((m, LANES), F32)] * 2,
        compiler_params=_params("parallel"), name="rope_tables",
    )(positions.reshape(m, 1), invf)


def _rglru_kernel(xr_ref, gr_ref, cw_ref, cb_ref, wa_ref, ba_ref, wx_ref, bx_ref, lam_ref, o_ref,
                  h_sc, tail_sc):
    @pl.when(pl.program_id(1) == 0)
    def _():
        h_sc[...] = jnp.zeros_like(h_sc)
        tail_sc[...] = jnp.zeros_like(tail_sc)

    t, width = xr_ref.shape
    xc, x = _causal_conv(xr_ref[...], tail_sc[...], cw_ref[...], _shift_matrices(t))
    xc = xc + cb_ref[...]
    tail_sc[...] = x[t - SUBLANES:]

    xcb = xc.astype(BF16)
    bw = width // RG_BLOCKS
    ra = jnp.concatenate([jnp.dot(xcb[:, n * bw:(n + 1) * bw], wa_ref[n], preferred_element_type=F32)
                          for n in range(RG_BLOCKS)], axis=1)
    rx = jnp.concatenate([jnp.dot(xcb[:, n * bw:(n + 1) * bw], wx_ref[n], preferred_element_type=F32)
                          for n in range(RG_BLOCKS)], axis=1)
    r = _sigmoid(ra + ba_ref[...])
    i = _sigmoid(rx + bx_ref[...])
    log_a = (-RG_C * _softplus(-lam_ref[...])) * r
    a = jnp.exp(log_a)
    th = jnp.tanh(log_a)
    b = jnp.sqrt(-2.0 * th / (1.0 - th)) * (i * xc)

    groups = t // SUBLANES
    a = a.reshape(groups, SUBLANES, width)
    b = b.reshape(groups, SUBLANES, width)
    sub = lax.broadcasted_iota(jnp.int32, (groups, SUBLANES, width), 1)
    s = 1
    while s < SUBLANES:
        keep = sub >= s
        b = jnp.where(keep, a * pltpu.roll(b, s, axis=1) + b, b)
        a = jnp.where(keep, a * pltpu.roll(a, s, axis=1), a)
        s *= 2
    carry = h_sc[0:1]
    hs = []
    for g in range(groups):
        hs.append(b[g] + a[g] * carry)
        carry = hs[-1][SUBLANES - 1:]
    h_sc[...] = jnp.broadcast_to(carry, h_sc.shape)
    h = jnp.concatenate(hs, axis=0)
    o_ref[...] = (h * jax.nn.gelu(gr_ref[...].astype(F32))).astype(o_ref.dtype)


def _rglru(x_r, gate_r, conv_w, conv_b, wa, ba, wx, bx, lam, batch, seq):
    width = x_r.shape[-1]
    tt = min(RG_TIME_TILE, seq)
    x_r = x_r.reshape(batch, seq, width)
    gate_r = gate_r.reshape(batch, seq, width)
    blk = pl.BlockSpec((None, tt, width), lambda b, t: (b, t, 0))
    vec = _resident((1, width))
    out = pl.pallas_call(
        _rglru_kernel, grid=(batch, seq // tt),
        in_specs=[blk, blk, _resident(conv_w.shape), vec, _resident(wa.shape), vec,
                  _resident(wx.shape), vec, vec],
        out_specs=blk,
        out_shape=jax.ShapeDtypeStruct((batch, seq, width), BF16),
        scratch_shapes=[pltpu.VMEM((SUBLANES, width), F32), pltpu.VMEM((SUBLANES, width), F32)],
        compiler_params=_params("parallel", "arbitrary"), name="rglru",
    )(x_r, gate_r, conv_w, conv_b.reshape(1, width), wa, ba.reshape(1, width), wx,
      bx.reshape(1, width), lam.reshape(1, width))
    return out.reshape(batch * seq, width)


ATTN_HEAD_WIDTH = 2 * LANES
LOG2_E = 1.4426950408889634


def _mla_proj_kernel(cq_ref, ckv_ref, cos_ref, sin_ref, qnw_ref, kvnw_ref, wq_ref, wk_ref, wv_ref,
                     q_o, k_o, v_o):
    tm = cq_ref.shape[0]
    hw = ATTN_HEAD_WIDTH
    half = MLA_ROPE // 2
    lane = lax.broadcasted_iota(jnp.int32, (tm, LANES), 1)
    cos = jnp.where(lane < MLA_ROPE, cos_ref[...], 0.0)
    sin = jnp.where(lane < MLA_ROPE, sin_ref[...], 0.0)

    def rope(x):
        rot = jnp.where(lane < half, -pltpu.roll(x, LANES - half, axis=1), pltpu.roll(x, half, axis=1))
        return x * cos + rot * sin

    q_scale = MLA_QK ** -0.5 * LOG2_E
    qn = _rms(cq_ref[...].astype(F32), qnw_ref[...]).astype(BF16)
    q = jnp.dot(qn, wq_ref[...], preferred_element_type=F32) * q_scale
    ckv = ckv_ref[...]
    kvn = _rms(ckv[:, :KV_LORA].astype(F32), kvnw_ref[...]).astype(BF16)
    k_nope = jnp.dot(kvn, wk_ref[...], preferred_element_type=F32)
    v = jnp.dot(kvn, wv_ref[...], preferred_element_type=F32)
    k_rope = rope(ckv[:, KV_LORA:].astype(F32)).astype(k_o.dtype)
    ones = jnp.ones((tm, LANES), v_o.dtype)
    for h in range(MLA_HEADS):
        lo, mid, hi = h * hw, h * hw + LANES, (h + 1) * hw
        q_o[:, lo:mid] = q[:, lo:mid].astype(q_o.dtype)
        q_o[:, mid:hi] = rope(q[:, mid:hi]).astype(q_o.dtype)
        k_o[:, lo:mid] = k_nope[:, h * MLA_NOPE:(h + 1) * MLA_NOPE].astype(k_o.dtype)
        k_o[:, mid:hi] = k_rope
        v_o[:, lo:mid] = v[:, h * MLA_V:(h + 1) * MLA_V].astype(v_o.dtype)
        v_o[:, mid:hi] = ones


def _mla_proj(c_q, ckv, cos, sin, q_norm, kv_norm, wq, wk, wv):
    m = c_q.shape[0]
    tm = min(ROW_TILE, m)
    row = lambda n: pl.BlockSpec((tm, n), lambda i: (i, 0))
    width = MLA_HEADS * ATTN_HEAD_WIDTH
    return pl.pallas_call(
        _mla_proj_kernel, grid=(m // tm,),
        in_specs=[row(c_q.shape[1]), row(ckv.shape[1]), row(LANES), row(LANES),
                  _resident((1, Q_LORA)), _resident((1, KV_LORA)), _resident(wq.shape),
                  _resident(wk.shape), _resident(wv.shape)],
        out_specs=[row(width)] * 3,
        out_shape=[jax.ShapeDtypeStruct((m, width), BF16)] * 3,
        compiler_params=_params("parallel"), name="mla_proj",
    )(c_q, ckv, cos, sin, q_norm.reshape(1, Q_LORA), kv_norm.reshape(1, KV_LORA), wq, wk, wv)


def _attn_kernel(q_ref, k_ref, v_ref, o_ref, acc_sc, *, tile):
    qi = pl.program_id(1)
    hw = ATTN_HEAD_WIDTH
    heads = range(MLA_HEADS)
    row = lax.broadcasted_iota(jnp.int32, (tile, tile), 0)
    col = lax.broadcasted_iota(jnp.int32, (tile, tile), 1)
    acc_sc[...] = jnp.zeros_like(acc_sc)

    def block(j, m_prev, masked):
        rows = pl.ds(pl.multiple_of(j * tile, tile), tile)
        s, m_new, alpha, pv = [], [], [], []
        for h in heads:
            x = lax.dot_general(q_ref[:, h * hw:(h + 1) * hw], k_ref[rows, h * hw:(h + 1) * hw],
                                (((1,), (1,)), ((), ())), preferred_element_type=F32)
            if masked:
                x = jnp.where(row >= col, x, -jnp.inf)
            s.append(x)
            m_new.append(jnp.maximum(m_prev[h], jnp.max(x, axis=-1, keepdims=True)))
        for h in heads:
            p = jnp.exp2(s[h] - m_new[h]).astype(BF16)
            pv.append(jnp.dot(p, v_ref[rows, h * hw:(h + 1) * hw], preferred_element_type=F32))
            alpha.append(jnp.exp2(m_prev[h] - m_new[h]))
        for h in heads:
            acc_sc[h] = alpha[h] * acc_sc[h] + pv[h]
        return m_new

    init = [jnp.full((tile, 1), -jnp.inf, F32) for _ in heads]
    m_run = lax.fori_loop(0, qi, lambda j, m: block(j, m, False), init)
    block(qi, m_run, True)
    for h in heads:
        acc = acc_sc[h]
        o_ref[:, h * MLA_V:(h + 1) * MLA_V] = (acc[:, :MLA_V] / acc[:, LANES:LANES + MLA_V]).astype(o_ref.dtype)


def _attention(q, k, v, batch, seq):
    tile = min(ATTN_TILE, seq)
    width = q.shape[-1]
    r3 = lambda a: a.reshape(batch, seq, width)
    kv_blk = pl.BlockSpec((None, seq, width), lambda b, t: (b, 0, 0), pipeline_mode=pl.Buffered(1))
    out = pl.pallas_call(
        functools.partial(_attn_kernel, tile=tile),
        grid=(batch, seq // tile),
        in_specs=[pl.BlockSpec((None, tile, width), lambda b, t: (b, t, 0)), kv_blk, kv_blk],
        out_specs=pl.BlockSpec((None, tile, MLA_HEADS * MLA_V), lambda b, t: (b, t, 0)),
        out_shape=jax.ShapeDtypeStruct((batch, seq, MLA_HEADS * MLA_V), BF16),
        scratch_shapes=[pltpu.VMEM((MLA_HEADS, tile, ATTN_HEAD_WIDTH), F32)],
        compiler_params=_params("parallel", "arbitrary"), name="mla_attention",
    )(r3(q), r3(k), r3(v))
    return out.reshape(batch * seq, MLA_HEADS * MLA_V)


def _gdn_in_proj_kernel(x_ref, nw_ref, wqkv_ref, wz_ref, wba_ref, cw_ref, qkv_o, z_o, ba_o, tail_sc,
                        *, tiles_per_seq):
    @pl.when(pl.program_id(0) % tiles_per_seq == 0)
    def _():
        tail_sc[...] = jnp.zeros_like(tail_sc)

    dk = GDN_HEAD_DIM
    tm = x_ref.shape[0]
    h = _rms(x_ref[...], nw_ref[...]).astype(BF16)
    slab = 2 * dk
    n_qk = 2 * GDN_K_HEADS * dk
    n_slabs = wqkv_ref.shape[1] // slab
    z_every = n_slabs * slab // wz_ref.shape[1]
    for j in range(n_slabs):
        c0 = j * slab
        cols = slice(c0, c0 + slab)
        tail = tail_sc[:, cols]
        for r0 in range(0, tm, GDN_PROJ_SUBTILE):
            rows = slice(r0, r0 + GDN_PROJ_SUBTILE)
            y = jnp.dot(h[rows], wqkv_ref[:, cols], preferred_element_type=F32)
            act = _silu(_causal_conv_vpu(y, tail, cw_ref[:, cols]))
            tail = y[GDN_PROJ_SUBTILE - SUBLANES:]
            for c1 in range(0, slab, dk):
                a = act[:, c1:c1 + dk]
                if c0 < n_qk:
                    scale = dk ** -0.5 if c0 < n_qk // 2 else 1.0
                    a = a * (lax.rsqrt(jnp.sum(a * a, axis=-1, keepdims=True) + NORM_EPS) * scale)
                qkv_o[rows, c0 + c1:c0 + c1 + dk] = a.astype(qkv_o.dtype)
        tail_sc[:, cols] = tail
        if j % z_every == z_every - 1:
            zc = slice((j // z_every) * slab, (j // z_every + 1) * slab)
            z_o[:, zc] = jnp.dot(h, wz_ref[:, zc], preferred_element_type=F32).astype(z_o.dtype)
    ba_o[...] = jnp.dot(h, wba_ref[...], preferred_element_type=F32)


def _gdn_in_proj(x, nw, w_qkv, w_z, w_ba, conv_w, seq):
    m, d = x.shape
    tm = min(ROW_TILE, seq)
    row = lambda n: pl.BlockSpec((tm, n), lambda i: (i, 0))
    widths = [w_qkv.shape[1], w_z.shape[1], w_ba.shape[1]]
    return pl.pallas_call(
        functools.partial(_gdn_in_proj_kernel, tiles_per_seq=seq // tm),
        grid=(m // tm,),
        in_specs=[row(d), _resident((1, d)), _resident(w_qkv.shape), _resident(w_z.shape),
                  _resident(w_ba.shape), _resident(conv_w.shape)],
        out_specs=[row(n) for n in widths],
        out_shape=[jax.ShapeDtypeStruct((m, n), dt) for n, dt in zip(widths, (BF16, BF16, F32))],
        scratch_shapes=[pltpu.VMEM((SUBLANES, w_qkv.shape[1]), F32)],
        compiler_params=_params("arbitrary"), name="gdn_in_proj",
    )(x, nw.reshape(1, d), w_qkv, w_z, w_ba, conv_w)


def _nilpotent_inverse_batch(mats, order, eye):
    xs = [eye - m for m in mats]
    ps = mats
    k = 1
    while 2 * k < order:
        ps = [_mm(p, p) for p in ps]
        xs = [x + _mm(x, p) for x, p in zip(xs, ps)]
        k *= 2
    return xs


def _unit_lower_inverse_batch(mats):
    c = mats[0].shape[0]
    r = lax.broadcasted_iota(jnp.int32, (c, c), 0)
    q = lax.broadcasted_iota(jnp.int32, (c, c), 1)
    eye = (r == q).astype(F32)
    same_block = (r // GDN_SOLVE_BLOCK) == (q // GDN_SOLVE_BLOCK)
    ds = [jnp.where(same_block, a, 0.0) for a in mats]
    lows = [a - d for a, d in zip(mats, ds)]
    xs = _nilpotent_inverse_batch(ds, GDN_SOLVE_BLOCK, eye)
    ns = [_mm(x, low) for x, low in zip(xs, lows)]
    ys = _nilpotent_inverse_batch(ns, c // GDN_SOLVE_BLOCK, eye)
    return [_mm(y, x) for y, x in zip(ys, xs)]


def _gdn_kernel(qkv_ref, z_ref, ba_ref, gp_ref, nw_ref, o_ref, s_sc, u_sc, wq_sc, kd_sc, qk_sc):
    @pl.when(pl.program_id(1) == 0)
    def _():
        s_sc[...] = jnp.zeros_like(s_sc)

    dk = GDN_HEAD_DIM
    n_kh, n_vh = GDN_K_HEADS, GDN_V_HEADS
    rep = n_vh // n_kh
    c = GDN_CHUNK_ROWS
    t = qkv_ref.shape[0]
    n_chunks = t // c

    def head_cols(j):
        return slice(j * dk, (j + 1) * dk)

    ba = ba_ref[...]
    lane = lax.broadcasted_iota(jnp.int32, ba.shape, 1)
    gates = jnp.where(lane >= n_vh, -jnp.exp(gp_ref[0:1]) * _softplus(ba + gp_ref[1:2]),
                      _sigmoid(ba))

    ri = lax.broadcasted_iota(jnp.int32, (c, c), 0)
    ci = lax.broadcasted_iota(jnp.int32, (c, c), 1)
    incl = (ci <= ri).astype(F32)
    incl_t = (ri <= ci).astype(F32)
    lower = ri >= ci
    strict = ri > ci

    rows_of = [slice(n * c, (n + 1) * c) for n in range(n_chunks)]
    gch = [gates[r] for r in rows_of]
    gc = [_mm_f32(incl, g) for g in gch]
    gc_t = [lax.dot_general(g, incl_t, (((0,), (0,)), ((), ())), preferred_element_type=F32,
                            precision=lax.Precision.HIGHEST) for g in gch]
    chunk_decay = {(n, h): jnp.exp(gc[n][c - 1:c, n_vh + h:n_vh + h + 1])
                   for n in range(n_chunks) for h in range(n_vh)}

    systems = [(n, h) for n in range(n_chunks) for h in range(n_vh)]
    for s0 in range(0, len(systems), GDN_SYSTEM_BATCH):
        batch = systems[s0:s0 + GDN_SYSTEM_BATCH]
        pairs = sorted({(n, h // rep) for n, h in batch})
        gcol = {(n, h): gc[n][:, n_vh + h:n_vh + h + 1] for n, h in batch}
        beta = {(n, h): gch[n][:, h:h + 1] for n, h in batch}
        decay = {}
        for n, h in batch:
            diff = gcol[n, h] - gc_t[n][n_vh + h:n_vh + h + 1, :]
            decay[n, h] = jnp.where(lower, jnp.exp(jnp.where(lower, diff, 0.0)), 0.0)
        q16 = {(n, kh): qkv_ref[rows_of[n], head_cols(kh)] for n, kh in pairs}
        k16 = {(n, kh): qkv_ref[rows_of[n], head_cols(n_kh + kh)] for n, kh in pairs}
        k32 = {p: k16[p].astype(F32) for p in pairs}
        kb = {(n, h): k32[n, h // rep] * beta[n, h] for n, h in batch}
        kk = {(n, h): _mm_nt(kb[n, h], k16[n, h // rep]) for n, h in batch}
        qk = {p: _mm_nt(q16[p], k16[p]) for p in pairs}
        t_inv = _unit_lower_inverse_batch([jnp.where(strict, kk[s] * decay[s], 0.0) for s in batch])
        egc = {s: jnp.exp(gcol[s]) for s in batch}
        uw = [_mm(ti, jnp.concatenate(
            [qkv_ref[rows_of[n], head_cols(2 * n_kh + h)].astype(F32) * beta[n, h], kb[n, h] * egc[n, h]], axis=1))
            for ti, (n, h) in zip(t_inv, batch)]
        for (n, h), uw_s in zip(batch, uw):
            rows = rows_of[n]
            u_sc[h, rows] = uw_s[:, :dk]
            wq_sc[h, n, :c] = uw_s[:, dk:].astype(BF16)
            wq_sc[h, n, c:] = (q16[n, h // rep].astype(F32) * egc[n, h]).astype(BF16)
            gl = gc[n][c - 1:c, n_vh + h:n_vh + h + 1]
            kd_sc[h, rows] = (k32[n, h // rep] * jnp.exp(gl - gcol[n, h])).astype(BF16)
            qk_sc[h, rows] = (qk[n, h // rep] * decay[n, h]).astype(BF16)

    states = [s_sc[h] for h in range(n_vh)]
    for n in range(n_chunks):
        rows = slice(n * c, (n + 1) * c)
        ws_qs = [jnp.dot(wq_sc[h, n], states[h].astype(BF16), preferred_element_type=F32) for h in range(n_vh)]
        v_new = [(u_sc[h, rows] - ws_qs[h][:c]).astype(BF16) for h in range(n_vh)]
        outs = [ws_qs[h][c:] + jnp.dot(qk_sc[h, rows], v_new[h], preferred_element_type=F32)
                for h in range(n_vh)]
        states = [states[h] * chunk_decay[n, h]
                  + lax.dot_general(kd_sc[h, rows], v_new[h], (((0,), (0,)), ((), ())),
                                    preferred_element_type=F32) for h in range(n_vh)]
        for h in range(n_vh):
            cols = slice(h * dk, (h + 1) * dk)
            o = _rms(outs[h], nw_ref[...]) * _silu(z_ref[rows, cols].astype(F32))
            o_ref[rows, cols] = o.astype(o_ref.dtype)
    for h in range(n_vh):
        s_sc[h] = states[h]


def _gdn_core(qkv, z, ba, a_log, dt_bias, norm_w, batch, seq):
    dk = GDN_HEAD_DIM
    n_kh, n_vh = GDN_K_HEADS, GDN_V_HEADS
    tt = min(GDN_TIME_TILE, seq)
    c = GDN_CHUNK_ROWS
    qkv = qkv.reshape(batch, seq, qkv.shape[-1])
    z = z.reshape(batch, seq, z.shape[-1])
    ba = ba.reshape(batch, seq, ba.shape[-1])
    pad = jnp.zeros((n_vh,), F32)
    gate_params = jnp.stack([jnp.concatenate([pad, a_log]), jnp.concatenate([pad, dt_bias])])
    blk = lambda n: pl.BlockSpec((None, tt, n), lambda b, t: (b, t, 0))
    out = pl.pallas_call(
        _gdn_kernel, grid=(batch, seq // tt),
        in_specs=[blk(qkv.shape[-1]), blk(z.shape[-1]), blk(ba.shape[-1]),
                  _resident(gate_params.shape), _resident((1, dk))],
        out_specs=blk(n_vh * dk),
        out_shape=jax.ShapeDtypeStruct((batch, seq, n_vh * dk), BF16),
        scratch_shapes=[pltpu.VMEM((n_vh, dk, dk), F32),
                        pltpu.VMEM((n_vh, tt, dk), F32),
                        pltpu.VMEM((n_vh, tt // c, 2 * c, dk), BF16),
                        pltpu.VMEM((n_vh, tt, dk), BF16),
                        pltpu.VMEM((n_vh, tt, c), BF16)],
        compiler_params=_params("parallel", "arbitrary"), name="gdn_core",
    )(qkv, z, ba, gate_params, norm_w.reshape(1, dk))
    return out.reshape(batch * seq, n_vh * dk)


def _hybrid_layer(x, cos, sin, npre, npost, w_in, conv_w, conv_b, gate_a_w, gate_a_b, gate_x_w, gate_x_b,
                  lam, q_norm, w_uq, kv_norm, w_ukv, w_out, batch, seq, x_is_ours):
    rg_w = lam.shape[0]
    o1, o2, o3, o4 = rg_w, 2 * rg_w, 2 * rg_w + Q_LORA, 2 * rg_w + Q_LORA + KV_LORA
    d_model = w_in.shape[0]
    w_ckv = jnp.concatenate([w_in[:, o3:], jnp.zeros((d_model, LANES - MLA_ROPE), w_in.dtype)], axis=1)
    weights = [w_in[:, :o1], w_in[:, o1:o2], w_in[:, o2:o3], w_ckv]
    x_r, gate_r, c_q, ckv = _norm_proj(x, npre, [w.astype(BF16) for w in weights], [BF16] * 4)

    y_a = _rglru(x_r, gate_r, conv_w, conv_b, gate_a_w.astype(BF16), gate_a_b, gate_x_w.astype(BF16),
                 gate_x_b, lam, batch, seq)

    wq = w_uq.reshape(Q_LORA, MLA_HEADS, MLA_QK)
    wq = jnp.concatenate([wq, jnp.zeros((Q_LORA, MLA_HEADS, ATTN_HEAD_WIDTH - MLA_QK), wq.dtype)], axis=2)
    wq = wq.reshape(Q_LORA, MLA_HEADS * ATTN_HEAD_WIDTH)
    wkv = w_ukv.reshape(KV_LORA, MLA_HEADS, MLA_NOPE + MLA_V)
    wk = wkv[:, :, :MLA_NOPE].reshape(KV_LORA, MLA_HEADS * MLA_NOPE)
    wv = wkv[:, :, MLA_NOPE:].reshape(KV_LORA, MLA_HEADS * MLA_V)
    q, k, v = _mla_proj(c_q, ckv, cos, sin, q_norm, kv_norm, wq.astype(BF16), wk.astype(BF16),
                        wv.astype(BF16))
    y_b = _attention(q, k, v, batch, seq)

    w_out = w_out.astype(BF16)
    return _out_proj([y_a, y_b], [w_out[:rg_w], w_out[rg_w:]], x, npost, in_place=x_is_ours)


def _gdn_layer(x, npre, npost, w_in, conv_w, a_log, dt_bias, norm_w, w_out, batch, seq):
    conv_c = conv_w.shape[1]
    v_w = GDN_V_HEADS * GDN_HEAD_DIM
    qkv, z, ba = _gdn_in_proj(x, npre, w_in[:, :conv_c].astype(BF16),
                              w_in[:, conv_c:conv_c + v_w].astype(BF16), w_in[:, conv_c + v_w:].astype(BF16),
                              conv_w, seq)
    y = _gdn_core(qkv, z, ba, a_log, dt_bias, norm_w, batch, seq)
    return _out_proj([y], [w_out.astype(BF16)], x, npost)


def kernel(x, positions, norm_mix_pre, norm_mix_post, norm_ffn_pre, norm_ffn_post,
           hy_w_in, rg_conv_w, rg_conv_b, rg_gate_a_w, rg_gate_a_b, rg_gate_x_w,
           rg_gate_x_b, rg_lambda, mla_q_norm, mla_w_uq, mla_kv_norm, mla_w_ukv, hy_w_out,
           gdn_w_in, gdn_conv_w, gdn_a_log, gdn_dt_bias, gdn_norm, gdn_w_out,
           ffn_w_gate, ffn_w_up, ffn_w_down):
    batch, seq, d = x.shape
    depth = norm_mix_pre.shape[0]
    cos, sin = _rope_tables(positions)
    wg_all, wu_all, wd_all = (w.astype(BF16) for w in (ffn_w_gate, ffn_w_up, ffn_w_down))
    x = x.reshape(batch * seq, d)
    for layer in range(depth):
        i = layer // 2
        if layer % 2 == 0:
            x = _hybrid_layer(x, cos, sin, norm_mix_pre[layer], norm_mix_post[layer], hy_w_in[i],
                              rg_conv_w[i], rg_conv_b[i], rg_gate_a_w[i], rg_gate_a_b[i], rg_gate_x_w[i],
                              rg_gate_x_b[i], rg_lambda[i], mla_q_norm[i], mla_w_uq[i], mla_kv_norm[i],
                              mla_w_ukv[i], hy_w_out[i], batch, seq, x_is_ours=layer > 0)
        else:
            x = _gdn_layer(x, norm_mix_pre[layer], norm_mix_post[layer], gdn_w_in[i], gdn_conv_w[i],
                           gdn_a_log[i], gdn_dt_bias[i], gdn_norm[i], gdn_w_out[i], batch, seq)
        x = _ffn(x, norm_ffn_pre[layer], wg_all, wu_all, wd_all, norm_ffn_post[layer], layer)
    return x.reshape(batch, seq, d)
```

```python
import functools

import jax
import jax.numpy as jnp
from jax import lax
from jax.experimental import pallas as pl
from jax.experimental.pallas import tpu as pltpu

F32 = jnp.float32
BF16 = jnp.bfloat16

NORM_EPS = 1e-6
LANES = 128
SUBLANES = 8
VMEM_LIMIT_BYTES = 48 * 1024 * 1024

CONV_WIDTH = 4
RG_BLOCKS = 8
RG_C = 8.0
MLA_HEADS = 8
MLA_NOPE = 128
MLA_ROPE = 64
MLA_V = 128
MLA_QK = MLA_NOPE + MLA_ROPE
Q_LORA = 512
KV_LORA = 256
ROPE_THETA = 10000.0
GDN_K_HEADS = 8
GDN_V_HEADS = 16
GDN_HEAD_DIM = 128
GDN_CHUNK_ROWS = 128
GDN_SOLVE_BLOCK = 16
GDN_PROJ_SUBTILE = 128
GDN_SYSTEM_BATCH = 16

ROW_TILE = 512
ATTN_TILE = 256
RG_ROW_TILE = 256
GDN_TIME_TILE = 256


def _params(*semantics):
    return pltpu.CompilerParams(dimension_semantics=semantics, vmem_limit_bytes=VMEM_LIMIT_BYTES)


def _rms(x, w):
    return x * lax.rsqrt(jnp.mean(x * x, axis=-1, keepdims=True) + NORM_EPS) * w


def _mm(a, b):
    return jnp.dot(a.astype(BF16), b.astype(BF16), preferred_element_type=F32)


def _mm_f32(a, b):
    return jnp.dot(a, b, preferred_element_type=F32, precision=lax.Precision.HIGHEST)


def _mm_nt(a, b):
    return lax.dot_general(a.astype(BF16), b.astype(BF16), (((1,), (1,)), ((), ())),
                           preferred_element_type=F32)


def _mm_tn(a, b):
    return lax.dot_general(a.astype(BF16), b.astype(BF16), (((0,), (0,)), ((), ())),
                           preferred_element_type=F32)


def _resident(shape):
    zeros = (0,) * len(shape)
    return pl.BlockSpec(shape, lambda *_: zeros, pipeline_mode=pl.Buffered(1))


def _causal_conv(x, prev_tail, w, batch):
    rows = x.shape[0]
    ext = jnp.concatenate([prev_tail, x], axis=0)
    y = x * w[CONV_WIDTH - 1:CONV_WIDTH]
    for s in range(1, CONV_WIDTH):
        off = (CONV_WIDTH - 1 - s) * batch
        y = y + ext[off:off + rows] * w[CONV_WIDTH - 1 - s:CONV_WIDTH - s]
    return y


def _sigmoid(x):
    return 0.5 + 0.5 * jnp.tanh(0.5 * x)


def _silu(x):
    h = 0.5 * x
    return h + h * jnp.tanh(h)


def _softplus(x):
    return jnp.maximum(x, 0.0) + jnp.log1p(jnp.exp(-jnp.abs(x)))


def _norm_proj_kernel(x_ref, nw_ref, *refs, n_out):
    w_refs, o_refs = refs[:n_out], refs[n_out:]
    h = _rms(x_ref[...], nw_ref[...]).astype(BF16)
    for w_ref, o_ref in zip(w_refs, o_refs):
        o_ref[...] = jnp.dot(h, w_ref[...], preferred_element_type=F32).astype(o_ref.dtype)


def _norm_proj(x, nw, weights, out_dtypes):
    m, d = x.shape
    tm = min(ROW_TILE, m)
    n_out = len(weights)
    in_specs = [pl.BlockSpec((tm, d), lambda i: (i, 0)), _resident((1, d))]
    in_specs += [_resident(w.shape) for w in weights]
    out_specs = [pl.BlockSpec((tm, w.shape[1]), lambda i: (i, 0)) for w in weights]
    out_shape = [jax.ShapeDtypeStruct((m, w.shape[1]), dt) for w, dt in zip(weights, out_dtypes)]
    return pl.pallas_call(
        functools.partial(_norm_proj_kernel, n_out=n_out),
        grid=(m // tm,), in_specs=in_specs, out_specs=out_specs, out_shape=out_shape,
        compiler_params=_params("parallel"), name="norm_proj",
    )(x, nw.reshape(1, d), *weights)


def _out_proj_kernel(*refs, n_in):
    a_refs, w_refs = refs[:n_in], refs[n_in:2 * n_in]
    x_ref, nw_ref, o_ref = refs[2 * n_in:]
    acc = jnp.dot(a_refs[0][...], w_refs[0][...], preferred_element_type=F32)
    for a_ref, w_ref in zip(a_refs[1:], w_refs[1:]):
        acc = acc + jnp.dot(a_ref[...], w_ref[...], preferred_element_type=F32)
    o_ref[...] = x_ref[...] + _rms(acc, nw_ref[...])


def _out_proj(acts, weights, x, nw):
    m, d = x.shape
    tm = min(ROW_TILE, m)
    n_in = len(acts)
    in_specs = [pl.BlockSpec((tm, a.shape[1]), lambda i: (i, 0)) for a in acts]
    in_specs += [_resident(w.shape) for w in weights]
    in_specs += [pl.BlockSpec((tm, d), lambda i: (i, 0)), _resident((1, d))]
    return pl.pallas_call(
        functools.partial(_out_proj_kernel, n_in=n_in),
        grid=(m // tm,), in_specs=in_specs,
        out_specs=pl.BlockSpec((tm, d), lambda i: (i, 0)),
        out_shape=jax.ShapeDtypeStruct((m, d), F32),
        input_output_aliases={2 * n_in: 0},
        compiler_params=_params("parallel"), name="out_proj",
    )(*acts, *weights, x, nw.reshape(1, d))


def _ffn_kernel(x_ref, npre_ref, wg_ref, wu_ref, wd_ref, npost_ref, o_ref, *, hidden_tile):
    x = x_ref[...]
    h = _rms(x, npre_ref[...]).astype(BF16)
    hidden = wg_ref.shape[1]
    acc = None
    for c0 in range(0, hidden, hidden_tile):
        g = jnp.dot(h, wg_ref[:, c0:c0 + hidden_tile], preferred_element_type=F32)
        u = jnp.dot(h, wu_ref[:, c0:c0 + hidden_tile], preferred_element_type=F32)
        a = (_silu(g) * u).astype(BF16)
        part = jnp.dot(a, wd_ref[c0:c0 + hidden_tile, :], preferred_element_type=F32)
        acc = part if acc is None else acc + part
    o_ref[...] = x + _rms(acc, npost_ref[...])


def _layer_slice(stacked, layer):
    shape = stacked.shape[1:]
    zeros = (0,) * len(shape)
    return pl.BlockSpec((None,) + shape, lambda *_: (layer,) + zeros, pipeline_mode=pl.Buffered(1))


def _ffn(x, npre, wg_all, wu_all, wd_all, npost, layer):
    m, d = x.shape
    tm = min(ROW_TILE, m)
    hidden = wg_all.shape[2]
    hidden_tile = hidden // 2 if (hidden // 2) % LANES == 0 else hidden
    return pl.pallas_call(
        functools.partial(_ffn_kernel, hidden_tile=hidden_tile),
        grid=(m // tm,),
        in_specs=[pl.BlockSpec((tm, d), lambda i: (i, 0)), _resident((1, d)),
                  _layer_slice(wg_all, layer), _layer_slice(wu_all, layer), _layer_slice(wd_all, layer),
                  _resident((1, d))],
        out_specs=pl.BlockSpec((tm, d), lambda i: (i, 0)),
        out_shape=jax.ShapeDtypeStruct((m, d), F32),
        input_output_aliases={0: 0},
        compiler_params=_params("parallel"), name="ffn",
    )(x, npre.reshape(1, d), wg_all, wu_all, wd_all, npost.reshape(1, d))


def _rope_table_kernel(pos_ref, invf_ref, cos_ref, sin_ref):
    ang = pos_ref[...].astype(F32) * invf_ref[...]
    cos_ref[...] = jnp.cos(ang)
    sin_ref[...] = jnp.sin(ang)


def _rope_tables(positions):
    m = positions.size
    tm = min(ROW_TILE, m)
    half = MLA_ROPE // 2
    inv_freq = 1.0 / (ROPE_THETA ** (jnp.arange(0, MLA_ROPE, 2, dtype=F32) / MLA_ROPE))
    invf = jnp.tile(inv_freq, LANES // half).reshape(1, LANES)
    return pl.pallas_call(
        _rope_table_kernel, grid=(m // tm,),
        in_specs=[pl.BlockSpec((tm, 1), lambda i: (i, 0)), _resident((1, LANES))],
        out_specs=[pl.BlockSpec((tm, LANES), lambda i: (i, 0))] * 2,
        out_shape=[jax.ShapeDtypeStruct((m, LANES), F32)] * 2,
        compiler_params=_params("parallel"), name="rope_tables",
    )(positions.reshape(m, 1), invf)


def _rglru_kernel(xr_ref, gr_ref, cw_ref, cb_ref, wa_ref, ba_ref, wx_ref, bx_ref, lam_ref, o_ref,
                  h_sc, tail_sc):
    @pl.when(pl.program_id(0) == 0)
    def _():
        h_sc[...] = jnp.zeros_like(h_sc)
        tail_sc[...] = jnp.zeros_like(tail_sc)

    rows, width = xr_ref.shape
    batch = h_sc.shape[0]
    x = xr_ref[...].astype(F32)
    xc = _causal_conv(x, tail_sc[...], cw_ref[...], batch) + cb_ref[...]
    tail_sc[...] = x[rows - tail_sc.shape[0]:]

    xcb = xc.astype(BF16)
    bw = width // RG_BLOCKS
    ra = jnp.concatenate([jnp.dot(xcb[:, n * bw:(n + 1) * bw], wa_ref[n], preferred_element_type=F32)
                          for n in range(RG_BLOCKS)], axis=1)
    rx = jnp.concatenate([jnp.dot(xcb[:, n * bw:(n + 1) * bw], wx_ref[n], preferred_element_type=F32)
                          for n in range(RG_BLOCKS)], axis=1)
    r = _sigmoid(ra + ba_ref[...])
    i = _sigmoid(rx + bx_ref[...])
    log_a = (-RG_C * _softplus(-lam_ref[...])) * r
    a = jnp.exp(log_a)
    th = jnp.tanh(log_a)
    b = jnp.sqrt(-2.0 * th / (1.0 - th)) * (i * xc)

    h = h_sc[...]
    hs = []
    for r0 in range(0, rows, batch):
        h = a[r0:r0 + batch] * h + b[r0:r0 + batch]
        hs.append(h)
    h_sc[...] = h
    o_ref[...] = (jnp.concatenate(hs, axis=0) * jax.nn.gelu(gr_ref[...].astype(F32))).astype(o_ref.dtype)


def _rglru(x_r, gate_r, conv_w, conv_b, wa, ba, wx, bx, lam, batch):
    m, width = x_r.shape
    rows = min(RG_ROW_TILE, m)
    blk = pl.BlockSpec((rows, width), lambda t: (t, 0))
    vec = _resident((1, width))
    return pl.pallas_call(
        _rglru_kernel, grid=(m // rows,),
        in_specs=[blk, blk, _resident(conv_w.shape), vec, _resident(wa.shape), vec,
                  _resident(wx.shape), vec, vec],
        out_specs=blk,
        out_shape=jax.ShapeDtypeStruct((m, width), BF16),
        scratch_shapes=[pltpu.VMEM((batch, width), F32),
                        pltpu.VMEM(((CONV_WIDTH - 1) * batch, width), F32)],
        compiler_params=_params("arbitrary"), name="rglru",
    )(x_r, gate_r, conv_w, conv_b.reshape(1, width), wa, ba.reshape(1, width), wx,
      bx.reshape(1, width), lam.reshape(1, width))


ATTN_HEAD_WIDTH = 2 * LANES
LOG2_E = 1.4426950408889634


def _mla_proj_kernel(cq_ref, ckv_ref, cos_ref, sin_ref, qnw_ref, kvnw_ref, wq_ref, wk_ref, wv_ref,
                     q_o, k_o, v_o):
    tm = cq_ref.shape[0]
    hw = ATTN_HEAD_WIDTH
    half = MLA_ROPE // 2
    lane = lax.broadcasted_iota(jnp.int32, (tm, LANES), 1)
    cos = jnp.where(lane < MLA_ROPE, cos_ref[...], 0.0)
    sin = jnp.where(lane < MLA_ROPE, sin_ref[...], 0.0)

    def rope(x):
        rot = jnp.where(lane < half, -pltpu.roll(x, LANES - half, axis=1), pltpu.roll(x, half, axis=1))
        return x * cos + rot * sin

    q_scale = MLA_QK ** -0.5 * LOG2_E
    qn = _rms(cq_ref[...].astype(F32), qnw_ref[...]).astype(BF16)
    q = jnp.dot(qn, wq_ref[...], preferred_element_type=F32) * q_scale
    ckv = ckv_ref[...]
    kvn = _rms(ckv[:, :KV_LORA].astype(F32), kvnw_ref[...]).astype(BF16)
    k_nope = jnp.dot(kvn, wk_ref[...], preferred_element_type=F32)
    v = jnp.dot(kvn, wv_ref[...], preferred_element_type=F32)
    k_rope = rope(ckv[:, KV_LORA:].astype(F32)).astype(k_o.dtype)
    ones = jnp.ones((tm, LANES), v_o.dtype)
    for h in range(MLA_HEADS):
        lo, mid, hi = h * hw, h * hw + LANES, (h + 1) * hw
        q_o[:, lo:mid] = q[:, lo:mid].astype(q_o.dtype)
        q_o[:, mid:hi] = rope(q[:, mid:hi]).astype(q_o.dtype)
        k_o[:, lo:mid] = k_nope[:, h * MLA_NOPE:(h + 1) * MLA_NOPE].astype(k_o.dtype)
        k_o[:, mid:hi] = k_rope
        v_o[:, lo:mid] = v[:, h * MLA_V:(h + 1) * MLA_V].astype(v_o.dtype)
        v_o[:, mid:hi] = ones


def _mla_proj(c_q, ckv, cos, sin, q_norm, kv_norm, wq, wk, wv):
    m = c_q.shape[0]
    tm = min(ROW_TILE, m)
    row = lambda n: pl.BlockSpec((tm, n), lambda i: (i, 0))
    width = MLA_HEADS * ATTN_HEAD_WIDTH
    return pl.pallas_call(
        _mla_proj_kernel, grid=(m // tm,),
        in_specs=[row(c_q.shape[1]), row(ckv.shape[1]), row(LANES), row(LANES),
                  _resident((1, Q_LORA)), _resident((1, KV_LORA)), _resident(wq.shape),
                  _resident(wk.shape), _resident(wv.shape)],
        out_specs=[row(width)] * 3,
        out_shape=[jax.ShapeDtypeStruct((m, width), BF16)] * 3,
        compiler_params=_params("parallel"), name="mla_proj",
    )(c_q, ckv, cos, sin, q_norm.reshape(1, Q_LORA), kv_norm.reshape(1, KV_LORA), wq, wk, wv)


def _attn_kernel(q_ref, k_ref, v_ref, o_ref, acc_sc, *, tile):
    qi = pl.program_id(1)
    hw = ATTN_HEAD_WIDTH
    heads = range(MLA_HEADS)
    row = lax.broadcasted_iota(jnp.int32, (tile, tile), 0)
    col = lax.broadcasted_iota(jnp.int32, (tile, tile), 1)
    acc_sc[...] = jnp.zeros_like(acc_sc)

    def block(j, m_prev, masked):
        rows = pl.ds(pl.multiple_of(j * tile, tile), tile)
        s, m_new, alpha, pv = [], [], [], []
        for h in heads:
            x = lax.dot_general(q_ref[:, h * hw:(h + 1) * hw], k_ref[rows, h * hw:(h + 1) * hw],
                                (((1,), (1,)), ((), ())), preferred_element_type=F32)
            if masked:
                x = jnp.where(row >= col, x, -jnp.inf)
            s.append(x)
            m_new.append(jnp.maximum(m_prev[h], jnp.max(x, axis=-1, keepdims=True)))
        for h in heads:
            p = jnp.exp2(s[h] - m_new[h]).astype(BF16)
            pv.append(jnp.dot(p, v_ref[rows, h * hw:(h + 1) * hw], preferred_element_type=F32))
            alpha.append(jnp.exp2(m_prev[h] - m_new[h]))
        for h in heads:
            acc_sc[h] = alpha[h] * acc_sc[h] + pv[h]
        return m_new

    init = [jnp.full((tile, 1), -jnp.inf, F32) for _ in heads]
    m_run = lax.fori_loop(0, qi, lambda j, m: block(j, m, False), init)
    block(qi, m_run, True)
    for h in heads:
        acc = acc_sc[h]
        o_ref[:, h * MLA_V:(h + 1) * MLA_V] = (acc[:, :MLA_V] / acc[:, LANES:LANES + MLA_V]).astype(o_ref.dtype)


def _attention(q, k, v, batch, seq):
    tile = min(ATTN_TILE, seq)
    width = q.shape[-1]
    out_width = MLA_HEADS * MLA_V
    per_seq = lambda a: a.reshape(seq, batch * width)
    kv_blk = pl.BlockSpec((seq, width), lambda b, t: (0, b), pipeline_mode=pl.Buffered(1))
    out = pl.pallas_call(
        functools.partial(_attn_kernel, tile=tile),
        grid=(batch, seq // tile),
        in_specs=[pl.BlockSpec((tile, width), lambda b, t: (t, b)), kv_blk, kv_blk],
        out_specs=pl.BlockSpec((tile, out_width), lambda b, t: (t, b)),
        out_shape=jax.ShapeDtypeStruct((seq, batch * out_width), BF16),
        scratch_shapes=[pltpu.VMEM((MLA_HEADS, tile, ATTN_HEAD_WIDTH), F32)],
        compiler_params=_params("parallel", "arbitrary"), name="mla_attention",
    )(per_seq(q), per_seq(k), per_seq(v))
    return out.reshape(seq * batch, out_width)


def _gdn_in_proj_kernel(x_ref, nw_ref, wqkv_ref, wz_ref, wba_ref, cw_ref, qkv_o, z_o, ba_o, tail_sc, *, batch):
    @pl.when(pl.program_id(0) == 0)
    def _():
        tail_sc[...] = jnp.zeros_like(tail_sc)

    dk = GDN_HEAD_DIM
    tm = x_ref.shape[0]
    n_tail = tail_sc.shape[0]
    h = _rms(x_ref[...], nw_ref[...]).astype(BF16)
    slab = 2 * dk
    n_qk = 2 * GDN_K_HEADS * dk
    n_slabs = wqkv_ref.shape[1] // slab
    z_every = n_slabs * slab // wz_ref.shape[1]
    for j in range(n_slabs):
        c0 = j * slab
        cols = slice(c0, c0 + slab)
        tail = tail_sc[:, cols]
        for r0 in range(0, tm, GDN_PROJ_SUBTILE):
            rows = slice(r0, r0 + GDN_PROJ_SUBTILE)
            y = jnp.dot(h[rows], wqkv_ref[:, cols], preferred_element_type=F32)
            act = _silu(_causal_conv(y, tail, cw_ref[:, cols], batch))
            tail = y[GDN_PROJ_SUBTILE - n_tail:]
            for c1 in range(0, slab, dk):
                a = act[:, c1:c1 + dk]
                if c0 < n_qk:
                    scale = dk ** -0.5 if c0 < n_qk // 2 else 1.0
                    a = a * (lax.rsqrt(jnp.sum(a * a, axis=-1, keepdims=True) + NORM_EPS) * scale)
                qkv_o[rows, c0 + c1:c0 + c1 + dk] = a.astype(qkv_o.dtype)
        tail_sc[:, cols] = tail
        if j % z_every == z_every - 1:
            zc = slice((j // z_every) * slab, (j // z_every + 1) * slab)
            z_o[:, zc] = jnp.dot(h, wz_ref[:, zc], preferred_element_type=F32).astype(z_o.dtype)
    ba_o[...] = jnp.dot(h, wba_ref[...], preferred_element_type=F32)


def _gdn_in_proj(x, nw, w_qkv, w_z, w_ba, conv_w, batch):
    m, d = x.shape
    tm = min(ROW_TILE, m)
    row = lambda n: pl.BlockSpec((tm, n), lambda i: (i, 0))
    widths = [w_qkv.shape[1], w_z.shape[1], w_ba.shape[1]]
    return pl.pallas_call(
        functools.partial(_gdn_in_proj_kernel, batch=batch),
        grid=(m // tm,),
        in_specs=[row(d), _resident((1, d)), _resident(w_qkv.shape), _resident(w_z.shape),
                  _resident(w_ba.shape), _resident(conv_w.shape)],
        out_specs=[row(n) for n in widths],
        out_shape=[jax.ShapeDtypeStruct((m, n), dt) for n, dt in zip(widths, (BF16, BF16, F32))],
        scratch_shapes=[pltpu.VMEM(((CONV_WIDTH - 1) * batch, w_qkv.shape[1]), F32)],
        compiler_params=_params("arbitrary"), name="gdn_in_proj",
    )(x, nw.reshape(1, d), w_qkv, w_z, w_ba, conv_w)


def _nilpotent_inverse_batch(mats, order, eye):
    xs = [eye - m for m in mats]
    ps = mats
    k = 1
    while 2 * k < order:
        ps = [_mm(p, p) for p in ps]
        xs = [x + _mm(x, p) for x, p in zip(xs, ps)]
        k *= 2
    return xs


def _unit_lower_inverse_batch(mats):
    c = mats[0].shape[0]
    r = lax.broadcasted_iota(jnp.int32, (c, c), 0)
    q = lax.broadcasted_iota(jnp.int32, (c, c), 1)
    eye = (r == q).astype(F32)
    same_block = (r // GDN_SOLVE_BLOCK) == (q // GDN_SOLVE_BLOCK)
    ds = [jnp.where(same_block, a, 0.0) for a in mats]
    lows = [a - d for a, d in zip(mats, ds)]
    xs = _nilpotent_inverse_batch(ds, GDN_SOLVE_BLOCK, eye)
    ns = [_mm(x, low) for x, low in zip(xs, lows)]
    ys = _nilpotent_inverse_batch(ns, c // GDN_SOLVE_BLOCK, eye)
    return [_mm(y, x) for y, x in zip(ys, xs)]


def _gdn_kernel(qkv_ref, z_ref, ba_ref, gp_ref, nw_ref, o_ref, s_sc, u_sc, wq_sc, kd_sc, qk_sc):
    @pl.when(pl.program_id(1) == 0)
    def _():
        s_sc[...] = jnp.zeros_like(s_sc)

    dk = GDN_HEAD_DIM
    n_kh, n_vh = GDN_K_HEADS, GDN_V_HEADS
    rep = n_vh // n_kh
    c = GDN_CHUNK_ROWS
    t = qkv_ref.shape[0]
    n_chunks = t // c

    def head_cols(j):
        return slice(j * dk, (j + 1) * dk)

    ba = ba_ref[...]
    lane = lax.broadcasted_iota(jnp.int32, ba.shape, 1)
    gates = jnp.where(lane >= n_vh, -jnp.exp(gp_ref[0:1]) * _softplus(ba + gp_ref[1:2]),
                      _sigmoid(ba))

    ri = lax.broadcasted_iota(jnp.int32, (c, c), 0)
    ci = lax.broadcasted_iota(jnp.int32, (c, c), 1)
    incl = (ci <= ri).astype(F32)
    incl_t = (ri <= ci).astype(F32)
    lower = ri >= ci
    strict = ri > ci

    rows_of = [slice(n * c, (n + 1) * c) for n in range(n_chunks)]
    gch = [gates[r] for r in rows_of]
    gc = [_mm_f32(incl, g) for g in gch]
    gc_t = [lax.dot_general(g, incl_t, (((0,), (0,)), ((), ())), preferred_element_type=F32,
                            precision=lax.Precision.HIGHEST) for g in gch]
    chunk_decay = {(n, h): jnp.exp(gc[n][c - 1:c, n_vh + h:n_vh + h + 1])
                   for n in range(n_chunks) for h in range(n_vh)}

    systems = [(n, h) for n in range(n_chunks) for h in range(n_vh)]
    for s0 in range(0, len(systems), GDN_SYSTEM_BATCH):
        batch = systems[s0:s0 + GDN_SYSTEM_BATCH]
        pairs = sorted({(n, h // rep) for n, h in batch})
        gcol = {(n, h): gc[n][:, n_vh + h:n_vh + h + 1] for n, h in batch}
        beta = {(n, h): gch[n][:, h:h + 1] for n, h in batch}
        decay = {}
        for n, h in batch:
            diff = gcol[n, h] - gc_t[n][n_vh + h:n_vh + h + 1, :]
            decay[n, h] = jnp.where(lower, jnp.exp(jnp.where(lower, diff, 0.0)), 0.0)
        q16 = {(n, kh): qkv_ref[rows_of[n], head_cols(kh)] for n, kh in pairs}
        k16 = {(n, kh): qkv_ref[rows_of[n], head_cols(n_kh + kh)] for n, kh in pairs}
        k32 = {p: k16[p].astype(F32) for p in pairs}
        kb = {(n, h): k32[n, h // rep] * beta[n, h] for n, h in batch}
        kk = {(n, h): _mm_nt(kb[n, h], k16[n, h // rep]) for n, h in batch}
        qk = {p: _mm_nt(q16[p], k16[p]) for p in pairs}
        t_inv = _unit_lower_inverse_batch([jnp.where(strict, kk[s] * decay[s], 0.0) for s in batch])
        egc = {s: jnp.exp(gcol[s]) for s in batch}
        uw = [_mm(ti, jnp.concatenate(
            [qkv_ref[rows_of[n], head_cols(2 * n_kh + h)].astype(F32) * beta[n, h], kb[n, h] * egc[n, h]], axis=1))
            for ti, (n, h) in zip(t_inv, batch)]
        for (n, h), uw_s in zip(batch, uw):
            rows = rows_of[n]
            u_sc[h, rows] = uw_s[:, :dk]
            wq_sc[h, n, :c] = uw_s[:, dk:].astype(BF16)
            wq_sc[h, n, c:] = (q16[n, h // rep].astype(F32) * egc[n, h]).astype(BF16)
            gl = gc[n][c - 1:c, n_vh + h:n_vh + h + 1]
            kd_sc[h, rows] = (k32[n, h // rep] * jnp.exp(gl - gcol[n, h])).astype(BF16)
            qk_sc[h, rows] = (qk[n, h // rep] * decay[n, h]).astype(BF16)

    states = [s_sc[h] for h in range(n_vh)]
    for n in range(n_chunks):
        rows = slice(n * c, (n + 1) * c)
        ws_qs = [jnp.dot(wq_sc[h, n], states[h].astype(BF16), preferred_element_type=F32) for h in range(n_vh)]
        v_new = [(u_sc[h, rows] - ws_qs[h][:c]).astype(BF16) for h in range(n_vh)]
        outs = [ws_qs[h][c:] + jnp.dot(qk_sc[h, rows], v_new[h], preferred_element_type=F32)
                for h in range(n_vh)]
        states = [states[h] * chunk_decay[n, h]
                  + lax.dot_general(kd_sc[h, rows], v_new[h], (((0,), (0,)), ((), ())),
                                    preferred_element_type=F32) for h in range(n_vh)]
        for h in range(n_vh):
            cols = slice(h * dk, (h + 1) * dk)
            o = _rms(outs[h], nw_ref[...]) * _silu(z_ref[rows, cols].astype(F32))
            o_ref[rows, cols] = o.astype(o_ref.dtype)
    for h in range(n_vh):
        s_sc[h] = states[h]


def _gdn_core(qkv, z, ba, a_log, dt_bias, norm_w, batch, seq):
    dk = GDN_HEAD_DIM
    n_kh, n_vh = GDN_K_HEADS, GDN_V_HEADS
    tt = min(GDN_TIME_TILE, seq)
    c = GDN_CHUNK_ROWS
    widths = [qkv.shape[-1], z.shape[-1], ba.shape[-1]]
    qkv, z, ba = (a.reshape(seq, batch * w) for a, w in zip((qkv, z, ba), widths))
    pad_lo = jnp.zeros((n_vh,), F32)
    pad_hi = jnp.zeros((widths[2] - 2 * n_vh,), F32)
    gate_params = jnp.stack([jnp.concatenate([pad_lo, a_log, pad_hi]),
                             jnp.concatenate([pad_lo, dt_bias, pad_hi])])
    blk = lambda n: pl.BlockSpec((tt, n), lambda b, t: (t, b))
    out = pl.pallas_call(
        _gdn_kernel, grid=(batch, seq // tt),
        in_specs=[blk(widths[0]), blk(widths[1]), blk(widths[2]),
                  _resident(gate_params.shape), _resident((1, dk))],
        out_specs=blk(n_vh * dk),
        out_shape=jax.ShapeDtypeStruct((seq, batch * n_vh * dk), BF16),
        scratch_shapes=[pltpu.VMEM((n_vh, dk, dk), F32),
                        pltpu.VMEM((n_vh, tt, dk), F32),
                        pltpu.VMEM((n_vh, tt // c, 2 * c, dk), BF16),
                        pltpu.VMEM((n_vh, tt, dk), BF16),
                        pltpu.VMEM((n_vh, tt, c), BF16)],
        compiler_params=_params("parallel", "arbitrary"), name="gdn_core",
    )(qkv, z, ba, gate_params, norm_w.reshape(1, dk))
    return out.reshape(seq * batch, n_vh * dk)


def _hybrid_layer(x, cos, sin, npre, npost, w_in, conv_w, conv_b, gate_a_w, gate_a_b, gate_x_w, gate_x_b,
                  lam, q_norm, w_uq, kv_norm, w_ukv, w_out, batch, seq):
    rg_w = lam.shape[0]
    o1, o2, o3, o4 = rg_w, 2 * rg_w, 2 * rg_w + Q_LORA, 2 * rg_w + Q_LORA + KV_LORA
    d_model = w_in.shape[0]
    w_ckv = jnp.concatenate([w_in[:, o3:], jnp.zeros((d_model, LANES - MLA_ROPE), w_in.dtype)], axis=1)
    weights = [w_in[:, :o1], w_in[:, o1:o2], w_in[:, o2:o3], w_ckv]
    x_r, gate_r, c_q, ckv = _norm_proj(x, npre, [w.astype(BF16) for w in weights], [BF16] * 4)

    y_a = _rglru(x_r, gate_r, conv_w, conv_b, gate_a_w.astype(BF16), gate_a_b, gate_x_w.astype(BF16),
                 gate_x_b, lam, batch)

    wq = w_uq.reshape(Q_LORA, MLA_HEADS, MLA_QK)
    wq = jnp.concatenate([wq, jnp.zeros((Q_LORA, MLA_HEADS, ATTN_HEAD_WIDTH - MLA_QK), wq.dtype)], axis=2)
    wq = wq.reshape(Q_LORA, MLA_HEADS * ATTN_HEAD_WIDTH)
    wkv = w_ukv.reshape(KV_LORA, MLA_HEADS, MLA_NOPE + MLA_V)
    wk = wkv[:, :, :MLA_NOPE].reshape(KV_LORA, MLA_HEADS * MLA_NOPE)
    wv = wkv[:, :, MLA_NOPE:].reshape(KV_LORA, MLA_HEADS * MLA_V)
    q, k, v = _mla_proj(c_q, ckv, cos, sin, q_norm, kv_norm, wq.astype(BF16), wk.astype(BF16),
                        wv.astype(BF16))
    y_b = _attention(q, k, v, batch, seq)

    w_out = w_out.astype(BF16)
    return _out_proj([y_a, y_b], [w_out[:rg_w], w_out[rg_w:]], x, npost)


def _gdn_layer(x, npre, npost, w_in, conv_w, a_log, dt_bias, norm_w, w_out, batch, seq):
    conv_c = conv_w.shape[1]
    v_w = GDN_V_HEADS * GDN_HEAD_DIM
    w_ba = w_in[:, conv_c + v_w:]
    w_ba = jnp.concatenate([w_ba, jnp.zeros((w_ba.shape[0], LANES - w_ba.shape[1]), w_ba.dtype)], axis=1)
    qkv, z, ba = _gdn_in_proj(x, npre, w_in[:, :conv_c].astype(BF16),
                              w_in[:, conv_c:conv_c + v_w].astype(BF16), w_ba.astype(BF16), conv_w, batch)
    y = _gdn_core(qkv, z, ba, a_log, dt_bias, norm_w, batch, seq)
    return _out_proj([y], [w_out.astype(BF16)], x, npost)


def kernel(x, positions, norm_mix_pre, norm_mix_post, norm_ffn_pre, norm_ffn_post,
           hy_w_in, rg_conv_w, rg_conv_b, rg_gate_a_w, rg_gate_a_b, rg_gate_x_w,
           rg_gate_x_b, rg_lambda, mla_q_norm, mla_w_uq, mla_kv_norm, mla_w_ukv, hy_w_out,
           gdn_w_in, gdn_conv_w, gdn_a_log, gdn_dt_bias, gdn_norm, gdn_w_out,
           ffn_w_gate, ffn_w_up, ffn_w_down):
    batch, seq, d = x.shape
    assert batch % SUBLANES == 0, "time-major row order keeps each time step's batch rows in whole vregs"
    depth = norm_mix_pre.shape[0]
    cos, sin = _rope_tables(positions.T)
    wg_all, wu_all, wd_all = (w.astype(BF16) for w in (ffn_w_gate, ffn_w_up, ffn_w_down))
    x = x.transpose(1, 0, 2).reshape(seq * batch, d)
    for layer in range(depth):
        i = layer // 2
        if layer % 2 == 0:
            x = _hybrid_layer(x, cos, sin, norm_mix_pre[layer], norm_mix_post[layer], hy_w_in[i],
                              rg_conv_w[i], rg_conv_b[i], rg_gate_a_w[i], rg_gate_a_b[i], rg_gate_x_w[i],
                              rg_gate_x_b[i], rg_lambda[i], mla_q_norm[i], mla_w_uq[i], mla_kv_norm[i],
                              mla_w_ukv[i], hy_w_out[i], batch, seq)
        else:
            x = _gdn_layer(x, norm_mix_pre[layer], norm_mix_post[layer], gdn_w_in[i], gdn_conv_w[i],
                           gdn_a_log[i], gdn_dt_bias[i], gdn_norm[i], gdn_w_out[i], batch, seq)
        x = _ffn(x, norm_ffn_pre[layer], wg_all, wu_all, wd_all, norm_ffn_post[layer], layer)
    return x.reshape(seq, batch, d).transpose(1, 0, 2)
```

```python
import functools

import jax
import jax.numpy as jnp
from jax import lax
from jax.experimental import pallas as pl
from jax.experimental.pallas import tpu as pltpu

F32 = jnp.float32
BF16 = jnp.bfloat16

NORM_EPS = 1e-6
LANES = 128
SUBLANES = 8
VMEM_LIMIT_BYTES = 48 * 1024 * 1024

CONV_WIDTH = 4
RG_BLOCKS = 8
RG_C = 8.0
MLA_HEADS = 8
MLA_NOPE = 128
MLA_ROPE = 64
MLA_V = 128
MLA_QK = MLA_NOPE + MLA_ROPE
Q_LORA = 512
KV_LORA = 256
ROPE_THETA = 10000.0
GDN_K_HEADS = 8
GDN_V_HEADS = 16
GDN_HEAD_DIM = 128
GDN_CHUNK_ROWS = 128
GDN_SOLVE_BLOCK = 16
GDN_PROJ_SUBTILE = 128
GDN_SYSTEM_BATCH = 16

ROW_TILE = 512
FFN_ROW_TILE = 1024
FFN_HIDDEN_TILE = 256
ATTN_TILE = 256
RG_TIME_TILE = 256
GDN_TIME_TILE = 256


def _params(*semantics):
    return pltpu.CompilerParams(dimension_semantics=semantics, vmem_limit_bytes=VMEM_LIMIT_BYTES)


def _rms(x, w):
    return x * lax.rsqrt(jnp.mean(x * x, axis=-1, keepdims=True) + NORM_EPS) * w


def _mm(a, b):
    return jnp.dot(a.astype(BF16), b.astype(BF16), preferred_element_type=F32)


def _mm_f32(a, b):
    return jnp.dot(a, b, preferred_element_type=F32, precision=lax.Precision.HIGHEST)


def _mm_nt(a, b):
    return lax.dot_general(a.astype(BF16), b.astype(BF16), (((1,), (1,)), ((), ())),
                           preferred_element_type=F32)


def _mm_tn(a, b):
    return lax.dot_general(a.astype(BF16), b.astype(BF16), (((0,), (0,)), ((), ())),
                           preferred_element_type=F32)


def _resident(shape):
    zeros = (0,) * len(shape)
    return pl.BlockSpec(shape, lambda *_: zeros, pipeline_mode=pl.Buffered(1))


def _shift_matrices(t):
    r = lax.broadcasted_iota(jnp.int32, (t, t), 0)
    c = lax.broadcasted_iota(jnp.int32, (t, t), 1)
    return [(r - c == s).astype(BF16) for s in range(1, CONV_WIDTH)]


def _causal_conv(x_bf16, prev_tail, w, shifts):
    x = x_bf16.astype(F32)
    y = x * w[CONV_WIDTH - 1:CONV_WIDTH]
    row = lax.broadcasted_iota(jnp.int32, prev_tail.shape, 0)
    head_fix = jnp.zeros(prev_tail.shape, F32)
    for s, shift in zip(range(1, CONV_WIDTH), shifts):
        w_s = w[CONV_WIDTH - 1 - s:CONV_WIDTH - s]
        y = y + jnp.dot(shift, x_bf16, preferred_element_type=F32) * w_s
        head_fix = head_fix + jnp.where(row < s, pltpu.roll(prev_tail, s, axis=0), 0.0) * w_s
    return jnp.concatenate([y[:SUBLANES] + head_fix, y[SUBLANES:]], axis=0), x


def _causal_conv_vpu(x, prev_tail, w):
    assert CONV_WIDTH == 4
    row = lax.broadcasted_iota(jnp.int32, prev_tail.shape, 0)

    def shift(v, v_tail, s):
        vs = pltpu.roll(v, s, axis=0)
        first = jnp.where(row < s, pltpu.roll(v_tail, s, axis=0), vs[:SUBLANES])
        return jnp.concatenate([first, vs[SUBLANES:]], axis=0)

    w0, w1, w2, w3 = (w[i:i + 1] for i in range(CONV_WIDTH))
    x1 = shift(x, prev_tail, 1)
    p = w1 * x + w0 * x1
    p_tail = w1 * prev_tail + w0 * pltpu.roll(prev_tail, 1, axis=0)
    return w3 * x + w2 * x1 + shift(p, p_tail, 2)


def _sigmoid(x):
    return 0.5 + 0.5 * jnp.tanh(0.5 * x)


def _silu(x):
    h = 0.5 * x
    return h + h * jnp.tanh(h)


def _softplus(x):
    return jnp.maximum(x, 0.0) + jnp.log1p(jnp.exp(-jnp.abs(x)))


def _norm_proj_kernel(x_ref, nw_ref, *refs, n_out):
    w_refs, o_refs = refs[:n_out], refs[n_out:]
    h = _rms(x_ref[...], nw_ref[...]).astype(BF16)
    for w_ref, o_ref in zip(w_refs, o_refs):
        o_ref[...] = jnp.dot(h, w_ref[...], preferred_element_type=F32).astype(o_ref.dtype)


def _norm_proj(x, nw, weights, out_dtypes):
    m, d = x.shape
    tm = min(ROW_TILE, m)
    n_out = len(weights)
    in_specs = [pl.BlockSpec((tm, d), lambda i: (i, 0)), _resident((1, d))]
    in_specs += [_resident(w.shape) for w in weights]
    out_specs = [pl.BlockSpec((tm, w.shape[1]), lambda i: (i, 0)) for w in weights]
    out_shape = [jax.ShapeDtypeStruct((m, w.shape[1]), dt) for w, dt in zip(weights, out_dtypes)]
    return pl.pallas_call(
        functools.partial(_norm_proj_kernel, n_out=n_out),
        grid=(m // tm,), in_specs=in_specs, out_specs=out_specs, out_shape=out_shape,
        compiler_params=_params("parallel"), name="norm_proj",
    )(x, nw.reshape(1, d), *weights)


def _out_proj_kernel(*refs, n_in):
    a_refs, w_refs = refs[:n_in], refs[n_in:2 * n_in]
    x_ref, nw_ref, o_ref = refs[2 * n_in:]
    acc = jnp.dot(a_refs[0][...], w_refs[0][...], preferred_element_type=F32)
    for a_ref, w_ref in zip(a_refs[1:], w_refs[1:]):
        acc = acc + jnp.dot(a_ref[...], w_ref[...], preferred_element_type=F32)
    o_ref[...] = x_ref[...] + _rms(acc, nw_ref[...])


def _out_proj(acts, weights, x, nw, in_place=True):
    m, d = x.shape
    tm = min(ROW_TILE, m)
    n_in = len(acts)
    in_specs = [pl.BlockSpec((tm, a.shape[1]), lambda i: (i, 0)) for a in acts]
    in_specs += [_resident(w.shape) for w in weights]
    in_specs += [pl.BlockSpec((tm, d), lambda i: (i, 0)), _resident((1, d))]
    return pl.pallas_call(
        functools.partial(_out_proj_kernel, n_in=n_in),
        grid=(m // tm,), in_specs=in_specs,
        out_specs=pl.BlockSpec((tm, d), lambda i: (i, 0)),
        out_shape=jax.ShapeDtypeStruct((m, d), F32),
        input_output_aliases={2 * n_in: 0} if in_place else {},
        compiler_params=_params("parallel"), name="out_proj",
    )(*acts, *weights, x, nw.reshape(1, d))


def _ffn_kernel(x_ref, npre_ref, wg_ref, wu_ref, wd_ref, npost_ref, o_ref, *, hidden_tile):
    x = x_ref[...]
    h = _rms(x, npre_ref[...]).astype(BF16)
    hidden = wg_ref.shape[1]
    acc = None
    for c0 in range(0, hidden, hidden_tile):
        g = jnp.dot(h, wg_ref[:, c0:c0 + hidden_tile], preferred_element_type=F32)
        u = jnp.dot(h, wu_ref[:, c0:c0 + hidden_tile], preferred_element_type=F32)
        a = (_silu(g) * u).astype(BF16)
        part = jnp.dot(a, wd_ref[c0:c0 + hidden_tile, :], preferred_element_type=F32)
        acc = part if acc is None else acc + part
    o_ref[...] = x + _rms(acc, npost_ref[...])


def _layer_slice(stacked, layer):
    shape = stacked.shape[1:]
    zeros = (0,) * len(shape)
    return pl.BlockSpec((None,) + shape, lambda *_: (layer,) + zeros, pipeline_mode=pl.Buffered(1))


def _ffn(x, npre, wg_all, wu_all, wd_all, npost, layer):
    m, d = x.shape
    tm = min(FFN_ROW_TILE, m)
    hidden = wg_all.shape[2]
    hidden_tile = FFN_HIDDEN_TILE if hidden % FFN_HIDDEN_TILE == 0 else hidden
    return pl.pallas_call(
        functools.partial(_ffn_kernel, hidden_tile=hidden_tile),
        grid=(m // tm,),
        in_specs=[pl.BlockSpec((tm, d), lambda i: (i, 0)), _resident((1, d)),
                  _layer_slice(wg_all, layer), _layer_slice(wu_all, layer), _layer_slice(wd_all, layer),
                  _resident((1, d))],
        out_specs=pl.BlockSpec((tm, d), lambda i: (i, 0)),
        out_shape=jax.ShapeDtypeStruct((m, d), F32),
        input_output_aliases={0: 0},
        compiler_params=_params("parallel"), name="ffn",
    )(x, npre.reshape(1, d), wg_all, wu_all, wd_all, npost.reshape(1, d))


def _rope_table_kernel(pos_ref, invf_ref, cos_ref, sin_ref):
    ang = pos_ref[...].astype(F32) * invf_ref[...]
    cos_ref[...] = jnp.cos(ang)
    sin_ref[...] = jnp.sin(ang)


def _rope_tables(positions):
    m = positions.size
    tm = min(ROW_TILE, m)
    half = MLA_ROPE // 2
    inv_freq = 1.0 / (ROPE_THETA ** (jnp.arange(0, MLA_ROPE, 2, dtype=F32) / MLA_ROPE))
    invf = jnp.tile(inv_freq, LANES // half).reshape(1, LANES)
    return pl.pallas_call(
        _rope_table_kernel, grid=(m // tm,),
        in_specs=[pl.BlockSpec((tm, 1), lambda i: (i, 0)), _resident((1, LANES))],
        out_specs=[pl.BlockSpec((tm, LANES), lambda i: (i, 0))] * 2,
        out_shape=[jax.ShapeDtypeStruct((m, LANES), F32)] * 2,
        compiler_params=_params("parallel"), name="rope_tables",
    )(positions.reshape(m, 1), invf)


def _rglru_kernel(xr_ref, gr_ref, cw_ref, cb_ref, wa_ref, ba_ref, wx_ref, bx_ref, lam_ref, o_ref,
                  h_sc, tail_sc):
    @pl.when(pl.program_id(1) == 0)
    def _():
        h_sc[...] = jnp.zeros_like(h_sc)
        tail_sc[...] = jnp.zeros_like(tail_sc)

    t, width = xr_ref.shape
    xc, x = _causal_conv(xr_ref[...], tail_sc[...], cw_ref[...], _shift_matrices(t))
    xc = xc + cb_ref[...]
    tail_sc[...] = x[t - SUBLANES:]

    xcb = xc.astype(BF16)
    bw = width // RG_BLOCKS
    ra = jnp.concatenate([jnp.dot(xcb[:, n * bw:(n + 1) * bw], wa_ref[n], preferred_element_type=F32)
                          for n in range(RG_BLOCKS)], axis=1)
    rx = jnp.concatenate([jnp.dot(xcb[:, n * bw:(n + 1) * bw], wx_ref[n], preferred_element_type=F32)
                          for n in range(RG_BLOCKS)], axis=1)
    r = _sigmoid(ra + ba_ref[...])
    i = _sigmoid(rx + bx_ref[...])
    log_a = (-RG_C * _softplus(-lam_ref[...])) * r
    a = jnp.exp(log_a)
    th = jnp.tanh(log_a)
    b = jnp.sqrt(-2.0 * th / (1.0 - th)) * (i * xc)

    groups = t // SUBLANES
    a = a.reshape(groups, SUBLANES, width)
    b = b.reshape(groups, SUBLANES, width)
    sub = lax.broadcasted_iota(jnp.int32, (groups, SUBLANES, width), 1)
    s = 1
    while s < SUBLANES:
        keep = sub >= s
        b = jnp.where(keep, a * pltpu.roll(b, s, axis=1) + b, b)
        a = jnp.where(keep, a * pltpu.roll(a, s, axis=1), a)
        s *= 2
    carry = h_sc[0:1]
    hs = []
    for g in range(groups):
        hs.append(b[g] + a[g] * carry)
        carry = hs[-1][SUBLANES - 1:]
    h_sc[...] = jnp.broadcast_to(carry, h_sc.shape)
    h = jnp.concatenate(hs, axis=0)
    o_ref[...] = (h * jax.nn.gelu(gr_ref[...].astype(F32))).astype(o_ref.dtype)


def _rglru(x_r, gate_r, conv_w, conv_b, wa, ba, wx, bx, lam, batch, seq):
    width = x_r.shape[-1]
    tt = min(RG_TIME_TILE, seq)
    x_r = x_r.reshape(batch, seq, width)
    gate_r = gate_r.reshape(batch, seq, width)
    blk = pl.BlockSpec((None, tt, width), lambda b, t: (b, t, 0))
    vec = _resident((1, width))
    out = pl.pallas_call(
        _rglru_kernel, grid=(batch, seq // tt),
        in_specs=[blk, blk, _resident(conv_w.shape), vec, _resident(wa.shape), vec,
                  _resident(wx.shape), vec, vec],
        out_specs=blk,
        out_shape=jax.ShapeDtypeStruct((batch, seq, width), BF16),
        scratch_shapes=[pltpu.VMEM((SUBLANES, width), F32), pltpu.VMEM((SUBLANES, width), F32)],
        compiler_params=_params("parallel", "arbitrary"), name="rglru",
    )(x_r, gate_r, conv_w, conv_b.reshape(1, width), wa, ba.reshape(1, width), wx,
      bx.reshape(1, width), lam.reshape(1, width))
    return out.reshape(batch * seq, width)


ATTN_HEAD_WIDTH = 2 * LANES
LOG2_E = 1.4426950408889634


def _mla_proj_kernel(cq_ref, ckv_ref, cos_ref, sin_ref, qnw_ref, kvnw_ref, wq_ref, wk_ref, wv_ref,
                     q_o, k_o, v_o):
    tm = cq_ref.shape[0]
    hw = ATTN_HEAD_WIDTH
    half = MLA_ROPE // 2
    lane = lax.broadcasted_iota(jnp.int32, (tm, LANES), 1)
    cos = jnp.where(lane < MLA_ROPE, cos_ref[...], 0.0)
    sin = jnp.where(lane < MLA_ROPE, sin_ref[...], 0.0)

    def rope(x):
        rot = jnp.where(lane < half, -pltpu.roll(x, LANES - half, axis=1), pltpu.roll(x, half, axis=1))
        return x * cos + rot * sin

    q_scale = MLA_QK ** -0.5 * LOG2_E
    qn = _rms(cq_ref[...].astype(F32), qnw_ref[...]).astype(BF16)
    q = jnp.dot(qn, wq_ref[...], preferred_element_type=F32) * q_scale
    ckv = ckv_ref[...]
    kvn = _rms(ckv[:, :KV_LORA].astype(F32), kvnw_ref[...]).astype(BF16)
    k_nope = jnp.dot(kvn, wk_ref[...], preferred_element_type=F32)
    v = jnp.dot(kvn, wv_ref[...], preferred_element_type=F32)
    k_rope = rope(ckv[:, KV_LORA:].astype(F32)).astype(k_o.dtype)
    ones = jnp.ones((tm, LANES), v_o.dtype)
    for h in range(MLA_HEADS):
        lo, mid, hi = h * hw, h * hw + LANES, (h + 1) * hw
        q_o[:, lo:mid] = q[:, lo:mid].astype(q_o.dtype)
        q_o[:, mid:hi] = rope(q[:, mid:hi]).astype(q_o.dtype)
        k_o[:, lo:mid] = k_nope[:, h * MLA_NOPE:(h + 1) * MLA_NOPE].astype(k_o.dtype)
        k_o[:, mid:hi] = k_rope
        v_o[:, lo:mid] = v[:, h * MLA_V:(h + 1) * MLA_V].astype(v_o.dtype)
        v_o[:, mid:hi] = ones


def _mla_proj(c_q, ckv, cos, sin, q_norm, kv_norm, wq, wk, wv):
    m = c_q.shape[0]
    tm = min(ROW_TILE, m)
    row = lambda n: pl.BlockSpec((tm, n), lambda i: (i, 0))
    width = MLA_HEADS * ATTN_HEAD_WIDTH
    return pl.pallas_call(
        _mla_proj_kernel, grid=(m // tm,),
        in_specs=[row(c_q.shape[1]), row(ckv.shape[1]), row(LANES), row(LANES),
                  _resident((1, Q_LORA)), _resident((1, KV_LORA)), _resident(wq.shape),
                  _resident(wk.shape), _resident(wv.shape)],
        out_specs=[row(width)] * 3,
        out_shape=[jax.ShapeDtypeStruct((m, width), BF16)] * 3,
        compiler_params=_params("parallel"), name="mla_proj",
    )(c_q, ckv, cos, sin, q_norm.reshape(1, Q_LORA), kv_norm.reshape(1, KV_LORA), wq, wk, wv)


def _attn_kernel(q_ref, k_ref, v_ref, o_ref, acc_sc, *, tile):
    qi = pl.program_id(1)
    hw = ATTN_HEAD_WIDTH
    heads = range(MLA_HEADS)
    row = lax.broadcasted_iota(jnp.int32, (tile, tile), 0)
    col = lax.broadcasted_iota(jnp.int32, (tile, tile), 1)
    acc_sc[...] = jnp.zeros_like(acc_sc)

    def block(start, width, m_prev, masked):
        rows = pl.ds(pl.multiple_of(start, tile), width)
        s, m_new, alpha, pv = [], [], [], []
        for h in heads:
            x = lax.dot_general(q_ref[:, h * hw:(h + 1) * hw], k_ref[rows, h * hw:(h + 1) * hw],
                                (((1,), (1,)), ((), ())), preferred_element_type=F32)
            if masked:
                x = jnp.where(row >= col, x, -jnp.inf)
            s.append(x)
            m_new.append(jnp.maximum(m_prev[h], jnp.max(x, axis=-1, keepdims=True)))
        for h in heads:
            p = jnp.exp2(s[h] - m_new[h]).astype(BF16)
            pv.append(jnp.dot(p, v_ref[rows, h * hw:(h + 1) * hw], preferred_element_type=F32))
            alpha.append(jnp.exp2(m_prev[h] - m_new[h]))
        for h in heads:
            acc_sc[h] = alpha[h] * acc_sc[h] + pv[h]
        return m_new

    init = [jnp.full((tile, 1), -jnp.inf, F32) for _ in heads]
    pairs = qi // 2
    m_run = lax.fori_loop(0, pairs, lambda j, m: block(j * (2 * tile), 2 * tile, m, False), init)
    m_run = lax.fori_loop(2 * pairs, qi, lambda j, m: block(j * tile, tile, m, False), m_run)
    block(qi * tile, tile, m_run, True)
    for h in heads:
        acc = acc_sc[h]
        o_ref[:, h * MLA_V:(h + 1) * MLA_V] = (acc[:, :MLA_V] / acc[:, LANES:LANES + MLA_V]).astype(o_ref.dtype)


def _attention(q, k, v, batch, seq):
    tile = min(ATTN_TILE, seq)
    width = q.shape[-1]
    r3 = lambda a: a.reshape(batch, seq, width)
    kv_blk = pl.BlockSpec((None, seq, width), lambda b, t: (b, 0, 0), pipeline_mode=pl.Buffered(1))
    out = pl.pallas_call(
        functools.partial(_attn_kernel, tile=tile),
        grid=(batch, seq // tile),
        in_specs=[pl.BlockSpec((None, tile, width), lambda b, t: (b, t, 0)), kv_blk, kv_blk],
        out_specs=pl.BlockSpec((None, tile, MLA_HEADS * MLA_V), lambda b, t: (b, t, 0)),
        out_shape=jax.ShapeDtypeStruct((batch, seq, MLA_HEADS * MLA_V), BF16),
        scratch_shapes=[pltpu.VMEM((MLA_HEADS, tile, ATTN_HEAD_WIDTH), F32)],
        compiler_params=_params("parallel", "arbitrary"), name="mla_attention",
    )(r3(q), r3(k), r3(v))
    return out.reshape(batch * seq, MLA_HEADS * MLA_V)


def _gdn_in_proj_kernel(x_ref, nw_ref, wqkv_ref, wz_ref, wba_ref, cw_ref, qkv_o, z_o, ba_o, tail_sc,
                        *, tiles_per_seq):
    @pl.when(pl.program_id(0) % tiles_per_seq == 0)
    def _():
        tail_sc[...] = jnp.zeros_like(tail_sc)

    dk = GDN_HEAD_DIM
    tm = x_ref.shape[0]
    h = _rms(x_ref[...], nw_ref[...]).astype(BF16)
    slab = 2 * dk
    n_qk = 2 * GDN_K_HEADS * dk
    n_slabs = wqkv_ref.shape[1] // slab
    z_every = n_slabs * slab // wz_ref.shape[1]
    for j in range(n_slabs):
        c0 = j * slab
        cols = slice(c0, c0 + slab)
        tail = tail_sc[:, cols]
        for r0 in range(0, tm, GDN_PROJ_SUBTILE):
            rows = slice(r0, r0 + GDN_PROJ_SUBTILE)
            y = jnp.dot(h[rows], wqkv_ref[:, cols], preferred_element_type=F32)
            act = _silu(_causal_conv_vpu(y, tail, cw_ref[:, cols]))
            tail = y[GDN_PROJ_SUBTILE - SUBLANES:]
            for c1 in range(0, slab, dk):
                a = act[:, c1:c1 + dk]
                if c0 < n_qk:
                    scale = dk ** -0.5 if c0 < n_qk // 2 else 1.0
                    a = a * (lax.rsqrt(jnp.sum(a * a, axis=-1, keepdims=True) + NORM_EPS) * scale)
                qkv_o[rows, c0 + c1:c0 + c1 + dk] = a.astype(qkv_o.dtype)
        tail_sc[:, cols] = tail
        if j % z_every == z_every - 1:
            zc = slice((j // z_every) * slab, (j // z_every + 1) * slab)
            z_o[:, zc] = jnp.dot(h, wz_ref[:, zc], preferred_element_type=F32).astype(z_o.dtype)
    ba_o[...] = jnp.dot(h, wba_ref[...], preferred_element_type=F32)


def _gdn_in_proj(x, nw, w_qkv, w_z, w_ba, conv_w, seq):
    m, d = x.shape
    tm = min(ROW_TILE, seq)
    row = lambda n: pl.BlockSpec((tm, n), lambda i: (i, 0))
    widths = [w_qkv.shape[1], w_z.shape[1], w_ba.shape[1]]
    return pl.pallas_call(
        functools.partial(_gdn_in_proj_kernel, tiles_per_seq=seq // tm),
        grid=(m // tm,),
        in_specs=[row(d), _resident((1, d)), _resident(w_qkv.shape), _resident(w_z.shape),
                  _resident(w_ba.shape), _resident(conv_w.shape)],
        out_specs=[row(n) for n in widths],
        out_shape=[jax.ShapeDtypeStruct((m, n), dt) for n, dt in zip(widths, (BF16, BF16, F32))],
        scratch_shapes=[pltpu.VMEM((SUBLANES, w_qkv.shape[1]), F32)],
        compiler_params=_params("arbitrary"), name="gdn_in_proj",
    )(x, nw.reshape(1, d), w_qkv, w_z, w_ba, conv_w)


def _nilpotent_inverse_batch(mats, order, eye):
    xs = [eye - m for m in mats]
    ps = mats
    k = 1
    while 2 * k < order:
        ps = [_mm(p, p) for p in ps]
        xs = [x + _mm(x, p) for x, p in zip(xs, ps)]
        k *= 2
    return xs


def _unit_lower_inverse_batch(mats):
    c = mats[0].shape[0]
    r = lax.broadcasted_iota(jnp.int32, (c, c), 0)
    q = lax.broadcasted_iota(jnp.int32, (c, c), 1)
    eye = (r == q).astype(F32)
    same_block = (r // GDN_SOLVE_BLOCK) == (q // GDN_SOLVE_BLOCK)
    ds = [jnp.where(same_block, a, 0.0) for a in mats]
    lows = [a - d for a, d in zip(mats, ds)]
    xs = _nilpotent_inverse_batch(ds, GDN_SOLVE_BLOCK, eye)
    ns = [_mm(x, low) for x, low in zip(xs, lows)]
    ys = _nilpotent_inverse_batch(ns, c // GDN_SOLVE_BLOCK, eye)
    return [_mm(y, x) for y, x in zip(ys, xs)]


def _gdn_kernel(qkv_ref, z_ref, ba_ref, gp_ref, nw_ref, o_ref, s_sc, u_sc, wq_sc, kd_sc, qk_sc):
    @pl.when(pl.program_id(1) == 0)
    def _():
        s_sc[...] = jnp.zeros_like(s_sc)

    dk = GDN_HEAD_DIM
    n_kh, n_vh = GDN_K_HEADS, GDN_V_HEADS
    rep = n_vh // n_kh
    c = GDN_CHUNK_ROWS
    t = qkv_ref.shape[0]
    n_chunks = t // c

    def head_cols(j):
        return slice(j * dk, (j + 1) * dk)

    ba = ba_ref[...]
    lane = lax.broadcasted_iota(jnp.int32, ba.shape, 1)
    gates = jnp.where(lane >= n_vh, -jnp.exp(gp_ref[0:1]) * _softplus(ba + gp_ref[1:2]),
                      _sigmoid(ba))

    ri = lax.broadcasted_iota(jnp.int32, (c, c), 0)
    ci = lax.broadcasted_iota(jnp.int32, (c, c), 1)
    incl = (ci <= ri).astype(F32)
    incl_t = (ri <= ci).astype(F32)
    lower = ri >= ci
    strict = ri > ci

    rows_of = [slice(n * c, (n + 1) * c) for n in range(n_chunks)]
    gch = [gates[r] for r in rows_of]
    gc = [_mm_f32(incl, g) for g in gch]
    gc_t = [lax.dot_general(g, incl_t, (((0,), (0,)), ((), ())), preferred_element_type=F32,
                            precision=lax.Precision.HIGHEST) for g in gch]
    chunk_decay = {(n, h): jnp.exp(gc[n][c - 1:c, n_vh + h:n_vh + h + 1])
                   for n in range(n_chunks) for h in range(n_vh)}

    systems = [(n, h) for n in range(n_chunks) for h in range(n_vh)]
    for s0 in range(0, len(systems), GDN_SYSTEM_BATCH):
        batch = systems[s0:s0 + GDN_SYSTEM_BATCH]
        pairs = sorted({(n, h // rep) for n, h in batch})
        gcol = {(n, h): gc[n][:, n_vh + h:n_vh + h + 1] for n, h in batch}
        beta = {(n, h): gch[n][:, h:h + 1] for n, h in batch}
        decay = {}
        for n, h in batch:
            diff = gcol[n, h] - gc_t[n][n_vh + h:n_vh + h + 1, :]
            decay[n, h] = jnp.where(lower, jnp.exp(jnp.where(lower, diff, 0.0)), 0.0)
        q16 = {(n, kh): qkv_ref[rows_of[n], head_cols(kh)] for n, kh in pairs}
        k16 = {(n, kh): qkv_ref[rows_of[n], head_cols(n_kh + kh)] for n, kh in pairs}
        k32 = {p: k16[p].astype(F32) for p in pairs}
        kb = {(n, h): k32[n, h // rep] * beta[n, h] for n, h in batch}
        kk = {(n, h): _mm_nt(kb[n, h], k16[n, h // rep]) for n, h in batch}
        qk = {p: _mm_nt(q16[p], k16[p]) for p in pairs}
        t_inv = _unit_lower_inverse_batch([jnp.where(strict, kk[s] * decay[s], 0.0) for s in batch])
        egc = {s: jnp.exp(gcol[s]) for s in batch}
        uw = [_mm(ti, jnp.concatenate(
            [qkv_ref[rows_of[n], head_cols(2 * n_kh + h)].astype(F32) * beta[n, h], kb[n, h] * egc[n, h]], axis=1))
            for ti, (n, h) in zip(t_inv, batch)]
        for (n, h), uw_s in zip(batch, uw):
            rows = rows_of[n]
            u_sc[h, rows] = uw_s[:, :dk]
            wq_sc[h, n, :c] = uw_s[:, dk:].astype(BF16)
            wq_sc[h, n, c:] = (q16[n, h // rep].astype(F32) * egc[n, h]).astype(BF16)
            gl = gc[n][c - 1:c, n_vh + h:n_vh + h + 1]
            kd_sc[h, rows] = (k32[n, h // rep] * jnp.exp(gl - gcol[n, h])).astype(BF16)
            qk_sc[h, rows] = (qk[n, h // rep] * decay[n, h]).astype(BF16)

    states = [s_sc[h] for h in range(n_vh)]
    for n in range(n_chunks):
        rows = slice(n * c, (n + 1) * c)
        ws_qs = [jnp.dot(wq_sc[h, n], states[h].astype(BF16), preferred_element_type=F32) for h in range(n_vh)]
        v_new = [(u_sc[h, rows] - ws_qs[h][:c]).astype(BF16) for h in range(n_vh)]
        outs = [ws_qs[h][c:] + jnp.dot(qk_sc[h, rows], v_new[h], preferred_element_type=F32)
                for h in range(n_vh)]
        states = [states[h] * chunk_decay[n, h]
                  + lax.dot_general(kd_sc[h, rows], v_new[h], (((0,), (0,)), ((), ())),
                                    preferred_element_type=F32) for h in range(n_vh)]
        for h in range(n_vh):
            cols = slice(h * dk, (h + 1) * dk)
            o = _rms(outs[h], nw_ref[...]) * _silu(z_ref[rows, cols].astype(F32))
            o_ref[rows, cols] = o.astype(o_ref.dtype)
    for h in range(n_vh):
        s_sc[h] = states[h]


def _gdn_core(qkv, z, ba, a_log, dt_bias, norm_w, batch, seq):
    dk = GDN_HEAD_DIM
    n_kh, n_vh = GDN_K_HEADS, GDN_V_HEADS
    tt = min(GDN_TIME_TILE, seq)
    c = GDN_CHUNK_ROWS
    qkv = qkv.reshape(batch, seq, qkv.shape[-1])
    z = z.reshape(batch, seq, z.shape[-1])
    ba = ba.reshape(batch, seq, ba.shape[-1])
    pad = jnp.zeros((n_vh,), F32)
    gate_params = jnp.stack([jnp.concatenate([pad, a_log]), jnp.concatenate([pad, dt_bias])])
    blk = lambda n: pl.BlockSpec((None, tt, n), lambda b, t: (b, t, 0))
    out = pl.pallas_call(
        _gdn_kernel, grid=(batch, seq // tt),
        in_specs=[blk(qkv.shape[-1]), blk(z.shape[-1]), blk(ba.shape[-1]),
                  _resident(gate_params.shape), _resident((1, dk))],
        out_specs=blk(n_vh * dk),
        out_shape=jax.ShapeDtypeStruct((batch, seq, n_vh * dk), BF16),
        scratch_shapes=[pltpu.VMEM((n_vh, dk, dk), F32),
                        pltpu.VMEM((n_vh, tt, dk), F32),
                        pltpu.VMEM((n_vh, tt // c, 2 * c, dk), BF16),
                        pltpu.VMEM((n_vh, tt, dk), BF16),
                        pltpu.VMEM((n_vh, tt, c), BF16)],
        compiler_params=_params("parallel", "arbitrary"), name="gdn_core",
    )(qkv, z, ba, gate_params, norm_w.reshape(1, dk))
    return out.reshape(batch * seq, n_vh * dk)


def _hybrid_layer(x, cos, sin, npre, npost, w_in, conv_w, conv_b, gate_a_w, gate_a_b, gate_x_w, gate_x_b,
                  lam, q_norm, w_uq, kv_norm, w_ukv, w_out, batch, seq, x_is_ours):
    rg_w = lam.shape[0]
    o1, o2, o3, o4 = rg_w, 2 * rg_w, 2 * rg_w + Q_LORA, 2 * rg_w + Q_LORA + KV_LORA
    d_model = w_in.shape[0]
    w_ckv = jnp.concatenate([w_in[:, o3:], jnp.zeros((d_model, LANES - MLA_ROPE), w_in.dtype)], axis=1)
    weights = [w_in[:, :o1], w_in[:, o1:o2], w_in[:, o2:o3], w_ckv]
    x_r, gate_r, c_q, ckv = _norm_proj(x, npre, [w.astype(BF16) for w in weights], [BF16] * 4)

    y_a = _rglru(x_r, gate_r, conv_w, conv_b, gate_a_w.astype(BF16), gate_a_b, gate_x_w.astype(BF16),
                 gate_x_b, lam, batch, seq)

    wq = w_uq.reshape(Q_LORA, MLA_HEADS, MLA_QK)
    wq = jnp.concatenate([wq, jnp.zeros((Q_LORA, MLA_HEADS, ATTN_HEAD_WIDTH - MLA_QK), wq.dtype)], axis=2)
    wq = wq.reshape(Q_LORA, MLA_HEADS * ATTN_HEAD_WIDTH)
    wkv = w_ukv.reshape(KV_LORA, MLA_HEADS, MLA_NOPE + MLA_V)
    wk = wkv[:, :, :MLA_NOPE].reshape(KV_LORA, MLA_HEADS * MLA_NOPE)
    wv = wkv[:, :, MLA_NOPE:].reshape(KV_LORA, MLA_HEADS * MLA_V)
    q, k, v = _mla_proj(c_q, ckv, cos, sin, q_norm, kv_norm, wq.astype(BF16), wk.astype(BF16),
                        wv.astype(BF16))
    y_b = _attention(q, k, v, batch, seq)

    w_out = w_out.astype(BF16)
    return _out_proj([y_a, y_b], [w_out[:rg_w], w_out[rg_w:]], x, npost, in_place=x_is_ours)


def _gdn_layer(x, npre, npost, w_in, conv_w, a_log, dt_bias, norm_w, w_out, batch, seq):
    conv_c = conv_w.shape[1]
    v_w = GDN_V_HEADS * GDN_HEAD_DIM
    qkv, z, ba = _gdn_in_proj(x, npre, w_in[:, :conv_c].astype(BF16),
                              w_in[:, conv_c:conv_c + v_w].astype(BF16), w_in[:, conv_c + v_w:].astype(BF16),
                              conv_w, seq)
    y = _gdn_core(qkv, z, ba, a_log, dt_bias, norm_w, batch, seq)
    return _out_proj([y], [w_out.astype(BF16)], x, npost)


def kernel(x, positions, norm_mix_pre, norm_mix_post, norm_ffn_pre, norm_ffn_post,
           hy_w_in, rg_conv_w, rg_conv_b, rg_gate_a_w, rg_gate_a_b, rg_gate_x_w,
           rg_gate_x_b, rg_lambda, mla_q_norm, mla_w_uq, mla_kv_norm, mla_w_ukv, hy_w_out,
           gdn_w_in, gdn_conv_w, gdn_a_log, gdn_dt_bias, gdn_norm, gdn_w_out,
           ffn_w_gate, ffn_w_up, ffn_w_down):
    batch, seq, d = x.shape
    depth = norm_mix_pre.shape[0]
    cos, sin = _rope_tables(positions)
    wg_all, wu_all, wd_all = (w.astype(BF16) for w in (ffn_w_gate, ffn_w_up, ffn_w_down))
    x = x.reshape(batch * seq, d)
    for layer in range(depth):
        i = layer // 2
        if layer % 2 == 0:
            x = _hybrid_layer(x, cos, sin, norm_mix_pre[layer], norm_mix_post[layer], hy_w_in[i],
                              rg_conv_w[i], rg_conv_b[i], rg_gate_a_w[i], rg_gate_a_b[i], rg_gate_x_w[i],
                              rg_gate_x_b[i], rg_lambda[i], mla_q_norm[i], mla_w_uq[i], mla_kv_norm[i],
                              mla_w_ukv[i], hy_w_out[i], batch, seq, x_is_ours=layer > 0)
        else:
            x = _gdn_layer(x, norm_mix_pre[layer], norm_mix_post[layer], gdn_w_in[i], gdn_conv_w[i],
                           gdn_a_log[i], gdn_dt_bias[i], gdn_norm[i], gdn_w_out[i], batch, seq)
        x = _ffn(x, norm_ffn_pre[layer], wg_all, wu_all, wd_all, norm_ffn_post[layer], layer)
    return x.reshape(batch, seq, d)
```

```python
import functools

import jax
import jax.numpy as jnp
from jax import lax
from jax.experimental import pallas as pl
from jax.experimental.pallas import tpu as pltpu

F32 = jnp.float32
BF16 = jnp.bfloat16

NORM_EPS = 1e-6
LANES = 128
SUBLANES = 8
VMEM_LIMIT_BYTES = 48 * 1024 * 1024

CONV_WIDTH = 4
RG_BLOCKS = 8
RG_C = 8.0
MLA_HEADS = 8
MLA_NOPE = 128
MLA_ROPE = 64
MLA_V = 128
MLA_QK = MLA_NOPE + MLA_ROPE
Q_LORA = 512
KV_LORA = 256
ROPE_THETA = 10000.0
GDN_K_HEADS = 8
GDN_V_HEADS = 16
GDN_HEAD_DIM = 128
GDN_CHUNK_ROWS = 128
GDN_SOLVE_BLOCK = 16
GDN_PROJ_SUBTILE = 128
GDN_SYSTEM_BATCH = 16

ROW_TILE = 512
FFN_ROW_TILE = 1024
FFN_HIDDEN_TILE = 256
ATTN_TILE = 256
ATTN_BLOCK_MULTIPLES = (4, 2, 1)
RG_TIME_TILE = 256
GDN_TIME_TILE = 256


def _params(*semantics):
    return pltpu.CompilerParams(dimension_semantics=semantics, vmem_limit_bytes=VMEM_LIMIT_BYTES)


def _rms(x, w):
    return x * lax.rsqrt(jnp.mean(x * x, axis=-1, keepdims=True) + NORM_EPS) * w


def _mm(a, b):
    return jnp.dot(a.astype(BF16), b.astype(BF16), preferred_element_type=F32)


def _mm_f32(a, b):
    return jnp.dot(a, b, preferred_element_type=F32, precision=lax.Precision.HIGHEST)


def _mm_nt(a, b):
    return lax.dot_general(a.astype(BF16), b.astype(BF16), (((1,), (1,)), ((), ())),
                           preferred_element_type=F32)


def _mm_tn(a, b):
    return lax.dot_general(a.astype(BF16), b.astype(BF16), (((0,), (0,)), ((), ())),
                           preferred_element_type=F32)


def _resident(shape):
    zeros = (0,) * len(shape)
    return pl.BlockSpec(shape, lambda *_: zeros, pipeline_mode=pl.Buffered(1))


def _shift_matrices(t):
    r = lax.broadcasted_iota(jnp.int32, (t, t), 0)
    c = lax.broadcasted_iota(jnp.int32, (t, t), 1)
    return [(r - c == s).astype(BF16) for s in range(1, CONV_WIDTH)]


def _causal_conv(x_bf16, prev_tail, w, shifts):
    x = x_bf16.astype(F32)
    y = x * w[CONV_WIDTH - 1:CONV_WIDTH]
    row = lax.broadcasted_iota(jnp.int32, prev_tail.shape, 0)
    head_fix = jnp.zeros(prev_tail.shape, F32)
    for s, shift in zip(range(1, CONV_WIDTH), shifts):
        w_s = w[CONV_WIDTH - 1 - s:CONV_WIDTH - s]
        y = y + jnp.dot(shift, x_bf16, preferred_element_type=F32) * w_s
        head_fix = head_fix + jnp.where(row < s, pltpu.roll(prev_tail, s, axis=0), 0.0) * w_s
    return jnp.concatenate([y[:SUBLANES] + head_fix, y[SUBLANES:]], axis=0), x


def _causal_conv_vpu(x, prev_tail, w):
    assert CONV_WIDTH == 4
    row = lax.broadcasted_iota(jnp.int32, prev_tail.shape, 0)

    def shift(v, v_tail, s):
        vs = pltpu.roll(v, s, axis=0)
        first = jnp.where(row < s, pltpu.roll(v_tail, s, axis=0), vs[:SUBLANES])
        return jnp.concatenate([first, vs[SUBLANES:]], axis=0)

    w0, w1, w2, w3 = (w[i:i + 1] for i in range(CONV_WIDTH))
    x1 = shift(x, prev_tail, 1)
    p = w1 * x + w0 * x1
    p_tail = w1 * prev_tail + w0 * pltpu.roll(prev_tail, 1, axis=0)
    return w3 * x + w2 * x1 + shift(p, p_tail, 2)


def _sigmoid(x):
    return 0.5 + 0.5 * jnp.tanh(0.5 * x)


def _silu(x):
    h = 0.5 * x
    return h + h * jnp.tanh(h)


def _softplus(x):
    return jnp.maximum(x, 0.0) + jnp.log1p(jnp.exp(-jnp.abs(x)))


def _norm_proj_kernel(x_ref, nw_ref, *refs, n_out):
    w_refs, o_refs = refs[:n_out], refs[n_out:]
    h = _rms(x_ref[...], nw_ref[...]).astype(BF16)
    for w_ref, o_ref in zip(w_refs, o_refs):
        o_ref[...] = jnp.dot(h, w_ref[...], preferred_element_type=F32).astype(o_ref.dtype)


def _norm_proj(x, nw, weights, out_dtypes):
    m, d = x.shape
    tm = min(ROW_TILE, m)
    n_out = len(weights)
    in_specs = [pl.BlockSpec((tm, d), lambda i: (i, 0)), _resident((1, d))]
    in_specs += [_resident(w.shape) for w in weights]
    out_specs = [pl.BlockSpec((tm, w.shape[1]), lambda i: (i, 0)) for w in weights]
    out_shape = [jax.ShapeDtypeStruct((m, w.shape[1]), dt) for w, dt in zip(weights, out_dtypes)]
    return pl.pallas_call(
        functools.partial(_norm_proj_kernel, n_out=n_out),
        grid=(m // tm,), in_specs=in_specs, out_specs=out_specs, out_shape=out_shape,
        compiler_params=_params("parallel"), name="norm_proj",
    )(x, nw.reshape(1, d), *weights)


def _out_proj_kernel(*refs, n_in):
    a_refs, w_refs = refs[:n_in], refs[n_in:2 * n_in]
    x_ref, nw_ref, o_ref = refs[2 * n_in:]
    acc = jnp.dot(a_refs[0][...], w_refs[0][...], preferred_element_type=F32)
    for a_ref, w_ref in zip(a_refs[1:], w_refs[1:]):
        acc = acc + jnp.dot(a_ref[...], w_ref[...], preferred_element_type=F32)
    o_ref[...] = x_ref[...] + _rms(acc, nw_ref[...])


def _out_proj(acts, weights, x, nw, in_place=True):
    m, d = x.shape
    tm = min(ROW_TILE, m)
    n_in = len(acts)
    in_specs = [pl.BlockSpec((tm, a.shape[1]), lambda i: (i, 0)) for a in acts]
    in_specs += [_resident(w.shape) for w in weights]
    in_specs += [pl.BlockSpec((tm, d), lambda i: (i, 0)), _resident((1, d))]
    return pl.pallas_call(
        functools.partial(_out_proj_kernel, n_in=n_in),
        grid=(m // tm,), in_specs=in_specs,
        out_specs=pl.BlockSpec((tm, d), lambda i: (i, 0)),
        out_shape=jax.ShapeDtypeStruct((m, d), F32),
        input_output_aliases={2 * n_in: 0} if in_place else {},
        compiler_params=_params("parallel"), name="out_proj",
    )(*acts, *weights, x, nw.reshape(1, d))


def _ffn_kernel(x_ref, npre_ref, wg_ref, wu_ref, wd_ref, npost_ref, o_ref, *, hidden_tile):
    x = x_ref[...]
    h = _rms(x, npre_ref[...]).astype(BF16)
    hidden = wg_ref.shape[1]
    acc = None
    for c0 in range(0, hidden, hidden_tile):
        g = jnp.dot(h, wg_ref[:, c0:c0 + hidden_tile], preferred_element_type=F32)
        u = jnp.dot(h, wu_ref[:, c0:c0 + hidden_tile], preferred_element_type=F32)
        a = (_silu(g) * u).astype(BF16)
        part = jnp.dot(a, wd_ref[c0:c0 + hidden_tile, :], preferred_element_type=F32)
        acc = part if acc is None else acc + part
    o_ref[...] = x + _rms(acc, npost_ref[...])


def _layer_slice(stacked, layer):
    shape = stacked.shape[1:]
    zeros = (0,) * len(shape)
    return pl.BlockSpec((None,) + shape, lambda *_: (layer,) + zeros, pipeline_mode=pl.Buffered(1))


def _ffn(x, npre, wg_all, wu_all, wd_all, npost, layer):
    m, d = x.shape
    tm = min(FFN_ROW_TILE, m)
    hidden = wg_all.shape[2]
    hidden_tile = FFN_HIDDEN_TILE if hidden % FFN_HIDDEN_TILE == 0 else hidden
    return pl.pallas_call(
        functools.partial(_ffn_kernel, hidden_tile=hidden_tile),
        grid=(m // tm,),
        in_specs=[pl.BlockSpec((tm, d), lambda i: (i, 0)), _resident((1, d)),
                  _layer_slice(wg_all, layer), _layer_slice(wu_all, layer), _layer_slice(wd_all, layer),
                  _resident((1, d))],
        out_specs=pl.BlockSpec((tm, d), lambda i: (i, 0)),
        out_shape=jax.ShapeDtypeStruct((m, d), F32),
        input_output_aliases={0: 0},
        compiler_params=_params("parallel"), name="ffn",
    )(x, npre.reshape(1, d), wg_all, wu_all, wd_all, npost.reshape(1, d))


def _rope_table_kernel(pos_ref, invf_ref, cos_ref, sin_ref):
    ang = pos_ref[...].astype(F32) * invf_ref[...]
    cos_ref[...] = jnp.cos(ang)
    sin_ref[...] = jnp.sin(ang)


def _rope_tables(positions):
    m = positions.size
    tm = min(ROW_TILE, m)
    half = MLA_ROPE // 2
    inv_freq = 1.0 / (ROPE_THETA ** (jnp.arange(0, MLA_ROPE, 2, dtype=F32) / MLA_ROPE))
    invf = jnp.tile(inv_freq, LANES // half).reshape(1, LANES)
    return pl.pallas_call(
        _rope_table_kernel, grid=(m // tm,),
        in_specs=[pl.BlockSpec((tm, 1), lambda i: (i, 0)), _resident((1, LANES))],
        out_specs=[pl.BlockSpec((tm, LANES), lambda i: (i, 0))] * 2,
        out_shape=[jax.ShapeDtypeStruct((m, LANES), F32)] * 2,
        compiler_params=_params("parallel"), name="rope_tables",
    )(positions.reshape(m, 1), invf)


def _rglru_kernel(xr_ref, gr_ref, cw_ref, cb_ref, wa_ref, ba_ref, wx_ref, bx_ref, lam_ref, o_ref,
                  h_sc, tail_sc):
    @pl.when(pl.program_id(1) == 0)
    def _():
        h_sc[...] = jnp.zeros_like(h_sc)
        tail_sc[...] = jnp.zeros_like(tail_sc)

    t, width = xr_ref.shape
    xc, x = _causal_conv(xr_ref[...], tail_sc[...], cw_ref[...], _shift_matrices(t))
    xc = xc + cb_ref[...]
    tail_sc[...] = x[t - SUBLANES:]

    xcb = xc.astype(BF16)
    bw = width // RG_BLOCKS
    ra = jnp.concatenate([jnp.dot(xcb[:, n * bw:(n + 1) * bw], wa_ref[n], preferred_element_type=F32)
                          for n in range(RG_BLOCKS)], axis=1)
    rx = jnp.concatenate([jnp.dot(xcb[:, n * bw:(n + 1) * bw], wx_ref[n], preferred_element_type=F32)
                          for n in range(RG_BLOCKS)], axis=1)
    r = _sigmoid(ra + ba_ref[...])
    i = _sigmoid(rx + bx_ref[...])
    log_a = (-RG_C * _softplus(-lam_ref[...])) * r
    a = jnp.exp(log_a)
    th = jnp.tanh(log_a)
    b = jnp.sqrt(-2.0 * th / (1.0 - th)) * (i * xc)

    groups = t // SUBLANES
    a = a.reshape(groups, SUBLANES, width)
    b = b.reshape(groups, SUBLANES, width)
    sub = lax.broadcasted_iota(jnp.int32, (groups, SUBLANES, width), 1)
    s = 1
    while s < SUBLANES:
        keep = sub >= s
        b = jnp.where(keep, a * pltpu.roll(b, s, axis=1) + b, b)
        a = jnp.where(keep, a * pltpu.roll(a, s, axis=1), a)
        s *= 2
    carry = h_sc[0:1]
    hs = []
    for g in range(groups):
        hs.append(b[g] + a[g] * carry)
        carry = hs[-1][SUBLANES - 1:]
    h_sc[...] = jnp.broadcast_to(carry, h_sc.shape)
    h = jnp.concatenate(hs, axis=0)
    o_ref[...] = (h * jax.nn.gelu(gr_ref[...].astype(F32))).astype(o_ref.dtype)


def _rglru(x_r, gate_r, conv_w, conv_b, wa, ba, wx, bx, lam, batch, seq):
    width = x_r.shape[-1]
    tt = min(RG_TIME_TILE, seq)
    x_r = x_r.reshape(batch, seq, width)
    gate_r = gate_r.reshape(batch, seq, width)
    blk = pl.BlockSpec((None, tt, width), lambda b, t: (b, t, 0))
    vec = _resident((1, width))
    out = pl.pallas_call(
        _rglru_kernel, grid=(batch, seq // tt),
        in_specs=[blk, blk, _resident(conv_w.shape), vec, _resident(wa.shape), vec,
                  _resident(wx.shape), vec, vec],
        out_specs=blk,
        out_shape=jax.ShapeDtypeStruct((batch, seq, width), BF16),
        scratch_shapes=[pltpu.VMEM((SUBLANES, width), F32), pltpu.VMEM((SUBLANES, width), F32)],
        compiler_params=_params("parallel", "arbitrary"), name="rglru",
    )(x_r, gate_r, conv_w, conv_b.reshape(1, width), wa, ba.reshape(1, width), wx,
      bx.reshape(1, width), lam.reshape(1, width))
    return out.reshape(batch * seq, width)


ATTN_HEAD_WIDTH = 2 * LANES
LOG2_E = 1.4426950408889634


def _mla_proj_kernel(cq_ref, ckv_ref, cos_ref, sin_ref, qnw_ref, kvnw_ref, wq_ref, wk_ref, wv_ref,
                     q_o, k_o, v_o):
    tm = cq_ref.shape[0]
    hw = ATTN_HEAD_WIDTH
    half = MLA_ROPE // 2
    lane = lax.broadcasted_iota(jnp.int32, (tm, LANES), 1)
    cos = jnp.where(lane < MLA_ROPE, cos_ref[...], 0.0)
    sin = jnp.where(lane < MLA_ROPE, sin_ref[...], 0.0)

    def rope(x):
        rot = jnp.where(lane < half, -pltpu.roll(x, LANES - half, axis=1), pltpu.roll(x, half, axis=1))
        return x * cos + rot * sin

    q_scale = MLA_QK ** -0.5 * LOG2_E
    qn = _rms(cq_ref[...].astype(F32), qnw_ref[...]).astype(BF16)
    q = jnp.dot(qn, wq_ref[...], preferred_element_type=F32) * q_scale
    ckv = ckv_ref[...]
    kvn = _rms(ckv[:, :KV_LORA].astype(F32), kvnw_ref[...]).astype(BF16)
    k_nope = jnp.dot(kvn, wk_ref[...], preferred_element_type=F32)
    v = jnp.dot(kvn, wv_ref[...], preferred_element_type=F32)
    k_rope = rope(ckv[:, KV_LORA:].astype(F32)).astype(k_o.dtype)
    ones = jnp.ones((tm, LANES), v_o.dtype)
    for h in range(MLA_HEADS):
        lo, mid, hi = h * hw, h * hw + LANES, (h + 1) * hw
        q_o[:, lo:mid] = q[:, lo:mid].astype(q_o.dtype)
        q_o[:, mid:hi] = rope(q[:, mid:hi]).astype(q_o.dtype)
        k_o[:, lo:mid] = k_nope[:, h * MLA_NOPE:(h + 1) * MLA_NOPE].astype(k_o.dtype)
        k_o[:, mid:hi] = k_rope
        v_o[:, lo:mid] = v[:, h * MLA_V:(h + 1) * MLA_V].astype(v_o.dtype)
        v_o[:, mid:hi] = ones


def _mla_proj(c_q, ckv, cos, sin, q_norm, kv_norm, wq, wk, wv):
    m = c_q.shape[0]
    tm = min(ROW_TILE, m)
    row = lambda n: pl.BlockSpec((tm, n), lambda i: (i, 0))
    width = MLA_HEADS * ATTN_HEAD_WIDTH
    return pl.pallas_call(
        _mla_proj_kernel, grid=(m // tm,),
        in_specs=[row(c_q.shape[1]), row(ckv.shape[1]), row(LANES), row(LANES),
                  _resident((1, Q_LORA)), _resident((1, KV_LORA)), _resident(wq.shape),
                  _resident(wk.shape), _resident(wv.shape)],
        out_specs=[row(width)] * 3,
        out_shape=[jax.ShapeDtypeStruct((m, width), BF16)] * 3,
        compiler_params=_params("parallel"), name="mla_proj",
    )(c_q, ckv, cos, sin, q_norm.reshape(1, Q_LORA), kv_norm.reshape(1, KV_LORA), wq, wk, wv)


def _attn_kernel(q_ref, k_ref, v_ref, o_ref, acc_sc, *, tile):
    qi = pl.program_id(1)
    hw = ATTN_HEAD_WIDTH
    heads = range(MLA_HEADS)
    row = lax.broadcasted_iota(jnp.int32, (tile, tile), 0)
    col = lax.broadcasted_iota(jnp.int32, (tile, tile), 1)
    acc_sc[...] = jnp.zeros_like(acc_sc)

    def block(start, width, m_prev, masked):
        rows = pl.ds(pl.multiple_of(start, tile), width)
        s, m_new, alpha, pv = [], [], [], []
        for h in heads:
            x = lax.dot_general(q_ref[:, h * hw:(h + 1) * hw], k_ref[rows, h * hw:(h + 1) * hw],
                                (((1,), (1,)), ((), ())), preferred_element_type=F32)
            if masked:
                x = jnp.where(row >= col, x, -jnp.inf)
            s.append(x)
            m_new.append(jnp.maximum(m_prev[h], jnp.max(x, axis=-1, keepdims=True)))
        for h in heads:
            p = jnp.exp2(s[h] - m_new[h]).astype(BF16)
            pv.append(jnp.dot(p, v_ref[rows, h * hw:(h + 1) * hw], preferred_element_type=F32))
            alpha.append(jnp.exp2(m_prev[h] - m_new[h]))
        for h in heads:
            acc_sc[h] = alpha[h] * acc_sc[h] + pv[h]
        return m_new

    m_run = [jnp.full((tile, 1), -jnp.inf, F32) for _ in heads]
    done = 0
    for mult in ATTN_BLOCK_MULTIPLES:
        count = (qi - done) // mult
        m_run = lax.fori_loop(
            0, count,
            lambda j, m, done=done, mult=mult: block((done + j * mult) * tile, mult * tile, m, False), m_run)
        done = done + count * mult
    block(qi * tile, tile, m_run, True)
    for h in heads:
        acc = acc_sc[h]
        o_ref[:, h * MLA_V:(h + 1) * MLA_V] = (acc[:, :MLA_V] / acc[:, LANES:LANES + MLA_V]).astype(o_ref.dtype)


def _attention(q, k, v, batch, seq):
    tile = min(ATTN_TILE, seq)
    width = q.shape[-1]
    r3 = lambda a: a.reshape(batch, seq, width)
    kv_blk = pl.BlockSpec((None, seq, width), lambda b, t: (b, 0, 0))
    out = pl.pallas_call(
        functools.partial(_attn_kernel, tile=tile),
        grid=(batch, seq // tile),
        in_specs=[pl.BlockSpec((None, tile, width), lambda b, t: (b, t, 0)), kv_blk, kv_blk],
        out_specs=pl.BlockSpec((None, tile, MLA_HEADS * MLA_V), lambda b, t: (b, t, 0)),
        out_shape=jax.ShapeDtypeStruct((batch, seq, MLA_HEADS * MLA_V), BF16),
        scratch_shapes=[pltpu.VMEM((MLA_HEADS, tile, ATTN_HEAD_WIDTH), F32)],
        compiler_params=_params("parallel", "arbitrary"), name="mla_attention",
    )(r3(q), r3(k), r3(v))
    return out.reshape(batch * seq, MLA_HEADS * MLA_V)


def _gdn_in_proj_kernel(x_ref, nw_ref, wqkv_ref, wz_ref, wba_ref, cw_ref, qkv_o, z_o, ba_o, tail_sc,
                        *, tiles_per_seq):
    @pl.when(pl.program_id(0) % tiles_per_seq == 0)
    def _():
        tail_sc[...] = jnp.zeros_like(tail_sc)

    dk = GDN_HEAD_DIM
    tm = x_ref.shape[0]
    h = _rms(x_ref[...], nw_ref[...]).astype(BF16)
    slab = 2 * dk
    n_qk = 2 * GDN_K_HEADS * dk
    n_slabs = wqkv_ref.shape[1] // slab
    z_every = n_slabs * slab // wz_ref.shape[1]
    for j in range(n_slabs):
        c0 = j * slab
        cols = slice(c0, c0 + slab)
        tail = tail_sc[:, cols]
        for r0 in range(0, tm, GDN_PROJ_SUBTILE):
            rows = slice(r0, r0 + GDN_PROJ_SUBTILE)
            y = jnp.dot(h[rows], wqkv_ref[:, cols], preferred_element_type=F32)
            act = _silu(_causal_conv_vpu(y, tail, cw_ref[:, cols]))
            tail = y[GDN_PROJ_SUBTILE - SUBLANES:]
            for c1 in range(0, slab, dk):
                a = act[:, c1:c1 + dk]
                if c0 < n_qk:
                    scale = dk ** -0.5 if c0 < n_qk // 2 else 1.0
                    a = a * (lax.rsqrt(jnp.sum(a * a, axis=-1, keepdims=True) + NORM_EPS) * scale)
                qkv_o[rows, c0 + c1:c0 + c1 + dk] = a.astype(qkv_o.dtype)
        tail_sc[:, cols] = tail
        if j % z_every == z_every - 1:
            zc = slice((j // z_every) * slab, (j // z_every + 1) * slab)
            z_o[:, zc] = jnp.dot(h, wz_ref[:, zc], preferred_element_type=F32).astype(z_o.dtype)
    ba_o[...] = jnp.dot(h, wba_ref[...], preferred_element_type=F32)


def _gdn_in_proj(x, nw, w_qkv, w_z, w_ba, conv_w, seq):
    m, d = x.shape
    tm = min(ROW_TILE, seq)
    row = lambda n: pl.BlockSpec((tm, n), lambda i: (i, 0))
    widths = [w_qkv.shape[1], w_z.shape[1], w_ba.shape[1]]
    return pl.pallas_call(
        functools.partial(_gdn_in_proj_kernel, tiles_per_seq=seq // tm),
        grid=(m // tm,),
        in_specs=[row(d), _resident((1, d)), _resident(w_qkv.shape), _resident(w_z.shape),
                  _resident(w_ba.shape), _resident(conv_w.shape)],
        out_specs=[row(n) for n in widths],
        out_shape=[jax.ShapeDtypeStruct((m, n), dt) for n, dt in zip(widths, (BF16, BF16, F32))],
        scratch_shapes=[pltpu.VMEM((SUBLANES, w_qkv.shape[1]), F32)],
        compiler_params=_params("arbitrary"), name="gdn_in_proj",
    )(x, nw.reshape(1, d), w_qkv, w_z, w_ba, conv_w)


def _nilpotent_inverse_batch(mats, order, eye):
    xs = [eye - m for m in mats]
    ps = mats
    k = 1
    while 2 * k < order:
        ps = [_mm(p, p) for p in ps]
        xs = [x + _mm(x, p) for x, p in zip(xs, ps)]
        k *= 2
    return xs


def _unit_lower_inverse_batch(mats):
    c = mats[0].shape[0]
    r = lax.broadcasted_iota(jnp.int32, (c, c), 0)
    q = lax.broadcasted_iota(jnp.int32, (c, c), 1)
    eye = (r == q).astype(F32)
    same_block = (r // GDN_SOLVE_BLOCK) == (q // GDN_SOLVE_BLOCK)
    ds = [jnp.where(same_block, a, 0.0) for a in mats]
    lows = [a - d for a, d in zip(mats, ds)]
    xs = _nilpotent_inverse_batch(ds, GDN_SOLVE_BLOCK, eye)
    ns = [_mm(x, low) for x, low in zip(xs, lows)]
    ys = _nilpotent_inverse_batch(ns, c // GDN_SOLVE_BLOCK, eye)
    return [_mm(y, x) for y, x in zip(ys, xs)]


def _gdn_kernel(qkv_ref, z_ref, ba_ref, gp_ref, nw_ref, o_ref, s_sc, u_sc, wq_sc, kd_sc, qk_sc):
    @pl.when(pl.program_id(1) == 0)
    def _():
        s_sc[...] = jnp.zeros_like(s_sc)

    dk = GDN_HEAD_DIM
    n_kh, n_vh = GDN_K_HEADS, GDN_V_HEADS
    rep = n_vh // n_kh
    c = GDN_CHUNK_ROWS
    t = qkv_ref.shape[0]
    n_chunks = t // c

    def head_cols(j):
        return slice(j * dk, (j + 1) * dk)

    ba = ba_ref[...]
    lane = lax.broadcasted_iota(jnp.int32, ba.shape, 1)
    gates = jnp.where(lane >= n_vh, -jnp.exp(gp_ref[0:1]) * _softplus(ba + gp_ref[1:2]),
                      _sigmoid(ba))

    ri = lax.broadcasted_iota(jnp.int32, (c, c), 0)
    ci = lax.broadcasted_iota(jnp.int32, (c, c), 1)
    incl = (ci <= ri).astype(F32)
    incl_t = (ri <= ci).astype(F32)
    lower = ri >= ci
    strict = ri > ci

    rows_of = [slice(n * c, (n + 1) * c) for n in range(n_chunks)]
    gch = [gates[r] for r in rows_of]
    gc = [_mm_f32(incl, g) for g in gch]
    gc_t = [lax.dot_general(g, incl_t, (((0,), (0,)), ((), ())), preferred_element_type=F32,
                            precision=lax.Precision.HIGHEST) for g in gch]
    chunk_decay = {(n, h): jnp.exp(gc[n][c - 1:c, n_vh + h:n_vh + h + 1])
                   for n in range(n_chunks) for h in range(n_vh)}

    systems = [(n, h) for n in range(n_chunks) for h in range(n_vh)]
    for s0 in range(0, len(systems), GDN_SYSTEM_BATCH):
        batch = systems[s0:s0 + GDN_SYSTEM_BATCH]
        pairs = sorted({(n, h // rep) for n, h in batch})
        gcol = {(n, h): gc[n][:, n_vh + h:n_vh + h + 1] for n, h in batch}
        beta = {(n, h): gch[n][:, h:h + 1] for n, h in batch}
        decay = {}
        for n, h in batch:
            diff = gcol[n, h] - gc_t[n][n_vh + h:n_vh + h + 1, :]
            decay[n, h] = jnp.where(lower, jnp.exp(jnp.where(lower, diff, 0.0)), 0.0)
        q16 = {(n, kh): qkv_ref[rows_of[n], head_cols(kh)] for n, kh in pairs}
        k16 = {(n, kh): qkv_ref[rows_of[n], head_cols(n_kh + kh)] for n, kh in pairs}
        k32 = {p: k16[p].astype(F32) for p in pairs}
        kb = {(n, h): k32[n, h // rep] * beta[n, h] for n, h in batch}
        kk = {(n, h): _mm_nt(kb[n, h], k16[n, h // rep]) for n, h in batch}
        qk = {p: _mm_nt(q16[p], k16[p]) for p in pairs}
        t_inv = _unit_lower_inverse_batch([jnp.where(strict, kk[s] * decay[s], 0.0) for s in batch])
        egc = {s: jnp.exp(gcol[s]) for s in batch}
        uw = [_mm(ti, jnp.concatenate(
            [qkv_ref[rows_of[n], head_cols(2 * n_kh + h)].astype(F32) * beta[n, h], kb[n, h] * egc[n, h]], axis=1))
            for ti, (n, h) in zip(t_inv, batch)]
        for (n, h), uw_s in zip(batch, uw):
            rows = rows_of[n]
            u_sc[h, rows] = uw_s[:, :dk]
            wq_sc[h, n, :c] = uw_s[:, dk:].astype(BF16)
            wq_sc[h, n, c:] = (q16[n, h // rep].astype(F32) * egc[n, h]).astype(BF16)
            gl = gc[n][c - 1:c, n_vh + h:n_vh + h + 1]
            kd_sc[h, rows] = (k32[n, h // rep] * jnp.exp(gl - gcol[n, h])).astype(BF16)
            qk_sc[h, rows] = (qk[n, h // rep] * decay[n, h]).astype(BF16)

    states = [s_sc[h] for h in range(n_vh)]
    for n in range(n_chunks):
        rows = slice(n * c, (n + 1) * c)
        ws_qs = [jnp.dot(wq_sc[h, n], states[h].astype(BF16), preferred_element_type=F32) for h in range(n_vh)]
        v_new = [(u_sc[h, rows] - ws_qs[h][:c]).astype(BF16) for h in range(n_vh)]
        outs = [ws_qs[h][c:] + jnp.dot(qk_sc[h, rows], v_new[h], preferred_element_type=F32)
                for h in range(n_vh)]
        states = [states[h] * chunk_decay[n, h]
                  + lax.dot_general(kd_sc[h, rows], v_new[h], (((0,), (0,)), ((), ())),
                                    preferred_element_type=F32) for h in range(n_vh)]
        for h in range(n_vh):
            cols = slice(h * dk, (h + 1) * dk)
            o = _rms(outs[h], nw_ref[...]) * _silu(z_ref[rows, cols].astype(F32))
            o_ref[rows, cols] = o.astype(o_ref.dtype)
    for h in range(n_vh):
        s_sc[h] = states[h]


def _gdn_core(qkv, z, ba, a_log, dt_bias, norm_w, batch, seq):
    dk = GDN_HEAD_DIM
    n_kh, n_vh = GDN_K_HEADS, GDN_V_HEADS
    tt = min(GDN_TIME_TILE, seq)
    c = GDN_CHUNK_ROWS
    qkv = qkv.reshape(batch, seq, qkv.shape[-1])
    z = z.reshape(batch, seq, z.shape[-1])
    ba = ba.reshape(batch, seq, ba.shape[-1])
    pad = jnp.zeros((n_vh,), F32)
    gate_params = jnp.stack([jnp.concatenate([pad, a_log]), jnp.concatenate([pad, dt_bias])])
    blk = lambda n: pl.BlockSpec((None, tt, n), lambda b, t: (b, t, 0))
    out = pl.pallas_call(
        _gdn_kernel, grid=(batch, seq // tt),
        in_specs=[blk(qkv.shape[-1]), blk(z.shape[-1]), blk(ba.shape[-1]),
                  _resident(gate_params.shape), _resident((1, dk))],
        out_specs=blk(n_vh * dk),
        out_shape=jax.ShapeDtypeStruct((batch, seq, n_vh * dk), BF16),
        scratch_shapes=[pltpu.VMEM((n_vh, dk, dk), F32),
                        pltpu.VMEM((n_vh, tt, dk), F32),
                        pltpu.VMEM((n_vh, tt // c, 2 * c, dk), BF16),
                        pltpu.VMEM((n_vh, tt, dk), BF16),
                        pltpu.VMEM((n_vh, tt, c), BF16)],
        compiler_params=_params("parallel", "arbitrary"), name="gdn_core",
    )(qkv, z, ba, gate_params, norm_w.reshape(1, dk))
    return out.reshape(batch * seq, n_vh * dk)


def _hybrid_layer(x, cos, sin, npre, npost, w_in, conv_w, conv_b, gate_a_w, gate_a_b, gate_x_w, gate_x_b,
                  lam, q_norm, w_uq, kv_norm, w_ukv, w_out, batch, seq, x_is_ours):
    rg_w = lam.shape[0]
    o1, o2, o3, o4 = rg_w, 2 * rg_w, 2 * rg_w + Q_LORA, 2 * rg_w + Q_LORA + KV_LORA
    d_model = w_in.shape[0]
    w_ckv = jnp.concatenate([w_in[:, o3:], jnp.zeros((d_model, LANES - MLA_ROPE), w_in.dtype)], axis=1)
    weights = [w_in[:, :o1], w_in[:, o1:o2], w_in[:, o2:o3], w_ckv]
    x_r, gate_r, c_q, ckv = _norm_proj(x, npre, [w.astype(BF16) for w in weights], [BF16] * 4)

    y_a = _rglru(x_r, gate_r, conv_w, conv_b, gate_a_w.astype(BF16), gate_a_b, gate_x_w.astype(BF16),
                 gate_x_b, lam, batch, seq)

    wq = w_uq.reshape(Q_LORA, MLA_HEADS, MLA_QK)
    wq = jnp.concatenate([wq, jnp.zeros((Q_LORA, MLA_HEADS, ATTN_HEAD_WIDTH - MLA_QK), wq.dtype)], axis=2)
    wq = wq.reshape(Q_LORA, MLA_HEADS * ATTN_HEAD_WIDTH)
    wkv = w_ukv.reshape(KV_LORA, MLA_HEADS, MLA_NOPE + MLA_V)
    wk = wkv[:, :, :MLA_NOPE].reshape(KV_LORA, MLA_HEADS * MLA_NOPE)
    wv = wkv[:, :, MLA_NOPE:].reshape(KV_LORA, MLA_HEADS * MLA_V)
    q, k, v = _mla_proj(c_q, ckv, cos, sin, q_norm, kv_norm, wq.astype(BF16), wk.astype(BF16),
                        wv.astype(BF16))
    y_b = _attention(q, k, v, batch, seq)

    w_out = w_out.astype(BF16)
    return _out_proj([y_a, y_b], [w_out[:rg_w], w_out[rg_w:]], x, npost, in_place=x_is_ours)


def _gdn_layer(x, npre, npost, w_in, conv_w, a_log, dt_bias, norm_w, w_out, batch, seq):
    conv_c = conv_w.shape[1]
    v_w = GDN_V_HEADS * GDN_HEAD_DIM
    qkv, z, ba = _gdn_in_proj(x, npre, w_in[:, :conv_c].astype(BF16),
                              w_in[:, conv_c:conv_c + v_w].astype(BF16), w_in[:, conv_c + v_w:].astype(BF16),
                              conv_w, seq)
    y = _gdn_core(qkv, z, ba, a_log, dt_bias, norm_w, batch, seq)
    return _out_proj([y], [w_out.astype(BF16)], x, npost)


def kernel(x, positions, norm_mix_pre, norm_mix_post, norm_ffn_pre, norm_ffn_post,
           hy_w_in, rg_conv_w, rg_conv_b, rg_gate_a_w, rg_gate_a_b, rg_gate_x_w,
           rg_gate_x_b, rg_lambda, mla_q_norm, mla_w_uq, mla_kv_norm, mla_w_ukv, hy_w_out,
           gdn_w_in, gdn_conv_w, gdn_a_log, gdn_dt_bias, gdn_norm, gdn_w_out,
           ffn_w_gate, ffn_w_up, ffn_w_down):
    batch, seq, d = x.shape
    depth = norm_mix_pre.shape[0]
    cos, sin = _rope_tables(positions)
    wg_all, wu_all, wd_all = (w.astype(BF16) for w in (ffn_w_gate, ffn_w_up, ffn_w_down))
    x = x.reshape(batch * seq, d)
    for layer in range(depth):
        i = layer // 2
        if layer % 2 == 0:
            x = _hybrid_layer(x, cos, sin, norm_mix_pre[layer], norm_mix_post[layer], hy_w_in[i],
                              rg_conv_w[i], rg_conv_b[i], rg_gate_a_w[i], rg_gate_a_b[i], rg_gate_x_w[i],
                              rg_gate_x_b[i], rg_lambda[i], mla_q_norm[i], mla_w_uq[i], mla_kv_norm[i],
                              mla_w_ukv[i], hy_w_out[i], batch, seq, x_is_ours=layer > 0)
        else:
            x = _gdn_layer(x, norm_mix_pre[layer], norm_mix_post[layer], gdn_w_in[i], gdn_conv_w[i],
                           gdn_a_log[i], gdn_dt_bias[i], gdn_norm[i], gdn_w_out[i], batch, seq)
        x = _ffn(x, norm_ffn_pre[layer], wg_all, wu_all, wd_all, norm_ffn_post[layer], layer)
    return x.reshape(batch, seq, d)
```

```python
import functools
from typing import NamedTuple

import jax
import jax.numpy as jnp
from jax import lax
from jax.experimental import pallas as pl
from jax.experimental.pallas import tpu as pltpu

F32 = jnp.float32
BF16 = jnp.bfloat16

NORM_EPS = 1e-6
LANES = 128
SUBLANES = 8
VMEM_LIMIT_BYTES = 48 * 1024 * 1024

CONV_WIDTH = 4
RG_BLOCKS = 8
RG_C = 8.0
MLA_HEADS = 8
MLA_NOPE = 128
MLA_ROPE = 64
MLA_V = 128
MLA_QK = MLA_NOPE + MLA_ROPE
Q_LORA = 512
KV_LORA = 256
ROPE_THETA = 10000.0
GDN_K_HEADS = 8
GDN_V_HEADS = 16
GDN_HEAD_DIM = 128
GDN_CHUNK_ROWS = 128
GDN_SOLVE_BLOCK = 16
GDN_PROJ_SUBTILE = 128
GDN_SYSTEM_BATCH = 16

ROW_TILE = 1024
FFN_ROW_TILE = 1024
FFN_HIDDEN_TILE = 256
ATTN_TILE = 256
ATTN_BLOCK_MULTIPLES = (4, 2, 1)
RG_TIME_TILE = 256
GDN_TIME_TILE = 256


def _params(*semantics):
    return pltpu.CompilerParams(dimension_semantics=semantics, vmem_limit_bytes=VMEM_LIMIT_BYTES)


def _rms(x, w):
    return x * lax.rsqrt(jnp.mean(x * x, axis=-1, keepdims=True) + NORM_EPS) * w


def _mm(a, b):
    return jnp.dot(a.astype(BF16), b.astype(BF16), preferred_element_type=F32)


def _mm_f32(a, b):
    return jnp.dot(a, b, preferred_element_type=F32, precision=lax.Precision.HIGHEST)


def _mm_nt(a, b):
    return lax.dot_general(a.astype(BF16), b.astype(BF16), (((1,), (1,)), ((), ())),
                           preferred_element_type=F32)


def _mm_tn(a, b):
    return lax.dot_general(a.astype(BF16), b.astype(BF16), (((0,), (0,)), ((), ())),
                           preferred_element_type=F32)


def _resident(shape):
    zeros = (0,) * len(shape)
    return pl.BlockSpec(shape, lambda *_: zeros, pipeline_mode=pl.Buffered(1))


class _Window(NamedTuple):
    array: jax.Array
    shape: tuple
    index: tuple


def _weight_spec(w):
    if isinstance(w, _Window):
        return pl.BlockSpec(w.shape, lambda *_: w.index, pipeline_mode=pl.Buffered(1))
    return _resident(w.shape)


def _weight_array(w):
    return w.array if isinstance(w, _Window) else w


def _weight_cols(w):
    return w.shape[-1]


def _shift_matrices(t):
    r = lax.broadcasted_iota(jnp.int32, (t, t), 0)
    c = lax.broadcasted_iota(jnp.int32, (t, t), 1)
    return [(r - c == s).astype(BF16) for s in range(1, CONV_WIDTH)]


def _causal_conv(x_bf16, prev_tail, w, shifts):
    x = x_bf16.astype(F32)
    y = x * w[CONV_WIDTH - 1:CONV_WIDTH]
    row = lax.broadcasted_iota(jnp.int32, prev_tail.shape, 0)
    head_fix = jnp.zeros(prev_tail.shape, F32)
    for s, shift in zip(range(1, CONV_WIDTH), shifts):
        w_s = w[CONV_WIDTH - 1 - s:CONV_WIDTH - s]
        y = y + jnp.dot(shift, x_bf16, preferred_element_type=F32) * w_s
        head_fix = head_fix + jnp.where(row < s, pltpu.roll(prev_tail, s, axis=0), 0.0) * w_s
    return jnp.concatenate([y[:SUBLANES] + head_fix, y[SUBLANES:]], axis=0), x


def _causal_conv_vpu(x, prev_tail, w):
    assert CONV_WIDTH == 4
    row = lax.broadcasted_iota(jnp.int32, prev_tail.shape, 0)

    def shift(v, v_tail, s):
        vs = pltpu.roll(v, s, axis=0)
        first = jnp.where(row < s, pltpu.roll(v_tail, s, axis=0), vs[:SUBLANES])
        return jnp.concatenate([first, vs[SUBLANES:]], axis=0)

    w0, w1, w2, w3 = (w[i:i + 1] for i in range(CONV_WIDTH))
    x1 = shift(x, prev_tail, 1)
    p = w1 * x + w0 * x1
    p_tail = w1 * prev_tail + w0 * pltpu.roll(prev_tail, 1, axis=0)
    return w3 * x + w2 * x1 + shift(p, p_tail, 2)


def _sigmoid(x):
    return 0.5 + 0.5 * jnp.tanh(0.5 * x)


def _silu(x):
    h = 0.5 * x
    return h + h * jnp.tanh(h)


def _softplus(x):
    return jnp.maximum(x, 0.0) + jnp.log1p(jnp.exp(-jnp.abs(x)))


def _norm_proj_kernel(x_ref, nw_ref, *refs, n_out):
    w_refs, o_refs = refs[:n_out], refs[n_out:]
    h = _rms(x_ref[...], nw_ref[...]).astype(BF16)
    for w_ref, o_ref in zip(w_refs, o_refs):
        o_ref[...] = jnp.dot(h, w_ref[...], preferred_element_type=F32).astype(o_ref.dtype)


def _norm_proj(x, nw, weights, out_dtypes):
    m, d = x.shape
    tm = min(ROW_TILE, m)
    n_out = len(weights)
    in_specs = [pl.BlockSpec((tm, d), lambda i: (i, 0)), _resident((1, d))]
    in_specs += [_weight_spec(w) for w in weights]
    out_specs = [pl.BlockSpec((tm, _weight_cols(w)), lambda i: (i, 0)) for w in weights]
    out_shape = [jax.ShapeDtypeStruct((m, _weight_cols(w)), dt) for w, dt in zip(weights, out_dtypes)]
    return pl.pallas_call(
        functools.partial(_norm_proj_kernel, n_out=n_out),
        grid=(m // tm,), in_specs=in_specs, out_specs=out_specs, out_shape=out_shape,
        compiler_params=_params("parallel"), name="norm_proj",
    )(x, nw.reshape(1, d), *[_weight_array(w) for w in weights])


def _out_proj_kernel(*refs, n_in):
    a_refs, w_refs = refs[:n_in], refs[n_in:2 * n_in]
    x_ref, nw_ref, o_ref = refs[2 * n_in:]
    acc = jnp.dot(a_refs[0][...], w_refs[0][...], preferred_element_type=F32)
    for a_ref, w_ref in zip(a_refs[1:], w_refs[1:]):
        acc = acc + jnp.dot(a_ref[...], w_ref[...], preferred_element_type=F32)
    o_ref[...] = x_ref[...] + _rms(acc, nw_ref[...])


def _out_proj(acts, weights, x, nw, in_place=True):
    m, d = x.shape
    tm = min(ROW_TILE, m)
    n_in = len(acts)
    in_specs = [pl.BlockSpec((tm, a.shape[1]), lambda i: (i, 0)) for a in acts]
    in_specs += [_weight_spec(w) for w in weights]
    in_specs += [pl.BlockSpec((tm, d), lambda i: (i, 0)), _resident((1, d))]
    return pl.pallas_call(
        functools.partial(_out_proj_kernel, n_in=n_in),
        grid=(m // tm,), in_specs=in_specs,
        out_specs=pl.BlockSpec((tm, d), lambda i: (i, 0)),
        out_shape=jax.ShapeDtypeStruct((m, d), F32),
        input_output_aliases={2 * n_in: 0} if in_place else {},
        compiler_params=_params("parallel"), name="out_proj",
    )(*acts, *[_weight_array(w) for w in weights], x, nw.reshape(1, d))


def _ffn_kernel(x_ref, npre_ref, wg_ref, wu_ref, wd_ref, npost_ref, o_ref, *, hidden_tile):
    x = x_ref[...]
    h = _rms(x, npre_ref[...]).astype(BF16)
    hidden = wg_ref.shape[1]
    acc = None
    for c0 in range(0, hidden, hidden_tile):
        g = jnp.dot(h, wg_ref[:, c0:c0 + hidden_tile], preferred_element_type=F32)
        u = jnp.dot(h, wu_ref[:, c0:c0 + hidden_tile], preferred_element_type=F32)
        a = (_silu(g) * u).astype(BF16)
        part = jnp.dot(a, wd_ref[c0:c0 + hidden_tile, :], preferred_element_type=F32)
        acc = part if acc is None else acc + part
    o_ref[...] = x + _rms(acc, npost_ref[...])


def _layer_window(stacked, layer):
    return _Window(stacked, (None,) + stacked.shape[1:], (layer, 0, 0))


def _ffn(x, npre, wg_all, wu_all, wd_all, npost, layer):
    m, d = x.shape
    tm = min(FFN_ROW_TILE, m)
    hidden = wg_all.shape[2]
    windows = [_layer_window(w, layer) for w in (wg_all, wu_all, wd_all)]
    hidden_tile = FFN_HIDDEN_TILE if hidden % FFN_HIDDEN_TILE == 0 else hidden
    return pl.pallas_call(
        functools.partial(_ffn_kernel, hidden_tile=hidden_tile),
        grid=(m // tm,),
        in_specs=[pl.BlockSpec((tm, d), lambda i: (i, 0)), _resident((1, d)),
                  *[_weight_spec(w) for w in windows], _resident((1, d))],
        out_specs=pl.BlockSpec((tm, d), lambda i: (i, 0)),
        out_shape=jax.ShapeDtypeStruct((m, d), F32),
        input_output_aliases={0: 0},
        compiler_params=_params("parallel"), name="ffn",
    )(x, npre.reshape(1, d), wg_all, wu_all, wd_all, npost.reshape(1, d))


def _rope_table_kernel(pos_ref, invf_ref, cos_ref, sin_ref):
    ang = pos_ref[...].astype(F32) * invf_ref[...]
    cos_ref[...] = jnp.cos(ang)
    sin_ref[...] = jnp.sin(ang)


def _rope_tables(positions):
    m = positions.size
    tm = min(ROW_TILE, m)
    half = MLA_ROPE // 2
    inv_freq = 1.0 / (ROPE_THETA ** (jnp.arange(0, MLA_ROPE, 2, dtype=F32) / MLA_ROPE))
    invf = jnp.tile(inv_freq, LANES // half).reshape(1, LANES)
    return pl.pallas_call(
        _rope_table_kernel, grid=(m // tm,),
        in_specs=[pl.BlockSpec((tm, 1), lambda i: (i, 0)), _resident((1, LANES))],
        out_specs=[pl.BlockSpec((tm, LANES), lambda i: (i, 0))] * 2,
        out_shape=[jax.ShapeDtypeStruct((m, LANES), F32)] * 2,
        compiler_params=_params("parallel"), name="rope_tables",
    )(positions.reshape(m, 1), invf)


def _rglru_kernel(xr_ref, gr_ref, cw_ref, cb_ref, wa_ref, ba_ref, wx_ref, bx_ref, lam_ref, o_ref,
                  h_sc, tail_sc):
    @pl.when(pl.program_id(1) == 0)
    def _():
        h_sc[...] = jnp.zeros_like(h_sc)
        tail_sc[...] = jnp.zeros_like(tail_sc)

    t, width = xr_ref.shape
    xc, x = _causal_conv(xr_ref[...], tail_sc[...], cw_ref[...], _shift_matrices(t))
    xc = xc + cb_ref[...]
    tail_sc[...] = x[t - SUBLANES:]

    xcb = xc.astype(BF16)
    bw = width // RG_BLOCKS
    ra = jnp.concatenate([jnp.dot(xcb[:, n * bw:(n + 1) * bw], wa_ref[n], preferred_element_type=F32)
                          for n in range(RG_BLOCKS)], axis=1)
    rx = jnp.concatenate([jnp.dot(xcb[:, n * bw:(n + 1) * bw], wx_ref[n], preferred_element_type=F32)
                          for n in range(RG_BLOCKS)], axis=1)
    r = _sigmoid(ra + ba_ref[...])
    i = _sigmoid(rx + bx_ref[...])
    log_a = (-RG_C * _softplus(-lam_ref[...])) * r
    a = jnp.exp(log_a)
    th = jnp.tanh(log_a)
    b = jnp.sqrt(-2.0 * th / (1.0 - th)) * (i * xc)

    groups = t // SUBLANES
    a = a.reshape(groups, SUBLANES, width)
    b = b.reshape(groups, SUBLANES, width)
    sub = lax.broadcasted_iota(jnp.int32, (groups, SUBLANES, width), 1)
    s = 1
    while s < SUBLANES:
        keep = sub >= s
        b = jnp.where(keep, a * pltpu.roll(b, s, axis=1) + b, b)
        a = jnp.where(keep, a * pltpu.roll(a, s, axis=1), a)
        s *= 2
    carry = h_sc[0:1]
    hs = []
    for g in range(groups):
        hs.append(b[g] + a[g] * carry)
        carry = hs[-1][SUBLANES - 1:]
    h_sc[...] = jnp.broadcast_to(carry, h_sc.shape)
    h = jnp.concatenate(hs, axis=0)
    o_ref[...] = (h * jax.nn.gelu(gr_ref[...].astype(F32))).astype(o_ref.dtype)


def _rglru(x_r, gate_r, conv_w, conv_b, wa, ba, wx, bx, lam, batch, seq):
    width = x_r.shape[-1]
    tt = min(RG_TIME_TILE, seq)
    x_r = x_r.reshape(batch, seq, width)
    gate_r = gate_r.reshape(batch, seq, width)
    blk = pl.BlockSpec((None, tt, width), lambda b, t: (b, t, 0))
    vec = _resident((1, width))
    out = pl.pallas_call(
        _rglru_kernel, grid=(batch, seq // tt),
        in_specs=[blk, blk, _resident(conv_w.shape), vec, _resident(wa.shape), vec,
                  _resident(wx.shape), vec, vec],
        out_specs=blk,
        out_shape=jax.ShapeDtypeStruct((batch, seq, width), BF16),
        scratch_shapes=[pltpu.VMEM((SUBLANES, width), F32), pltpu.VMEM((SUBLANES, width), F32)],
        compiler_params=_params("parallel", "arbitrary"), name="rglru",
    )(x_r, gate_r, conv_w, conv_b.reshape(1, width), wa, ba.reshape(1, width), wx,
      bx.reshape(1, width), lam.reshape(1, width))
    return out.reshape(batch * seq, width)


ATTN_HEAD_WIDTH = 2 * LANES
LOG2_E = 1.4426950408889634


def _mla_proj_kernel(cq_ref, ckv_ref, cos_ref, sin_ref, qnw_ref, kvnw_ref, wq_ref, wk_ref, wv_ref,
                     q_o, k_o, v_o):
    tm = cq_ref.shape[0]
    hw = ATTN_HEAD_WIDTH
    half = MLA_ROPE // 2
    lane = lax.broadcasted_iota(jnp.int32, (tm, LANES), 1)
    cos = jnp.where(lane < MLA_ROPE, cos_ref[...], 0.0)
    sin = jnp.where(lane < MLA_ROPE, sin_ref[...], 0.0)

    def rope(x):
        rot = jnp.where(lane < half, -pltpu.roll(x, LANES - half, axis=1), pltpu.roll(x, half, axis=1))
        return x * cos + rot * sin

    q_scale = MLA_QK ** -0.5 * LOG2_E
    qn = _rms(cq_ref[...].astype(F32), qnw_ref[...]).astype(BF16)
    q = jnp.dot(qn, wq_ref[...], preferred_element_type=F32) * q_scale
    ckv = ckv_ref[...]
    kvn = _rms(ckv[:, :KV_LORA].astype(F32), kvnw_ref[...]).astype(BF16)
    k_nope = jnp.dot(kvn, wk_ref[...], preferred_element_type=F32)
    v = jnp.dot(kvn, wv_ref[...], preferred_element_type=F32)
    k_rope = rope(ckv[:, KV_LORA:].astype(F32)).astype(k_o.dtype)
    ones = jnp.ones((tm, LANES), v_o.dtype)
    for h in range(MLA_HEADS):
        lo, mid, hi = h * hw, h * hw + LANES, (h + 1) * hw
        q_o[:, lo:mid] = q[:, lo:mid].astype(q_o.dtype)
        q_o[:, mid:hi] = rope(q[:, mid:hi]).astype(q_o.dtype)
        k_o[:, lo:mid] = k_nope[:, h * MLA_NOPE:(h + 1) * MLA_NOPE].astype(k_o.dtype)
        k_o[:, mid:hi] = k_rope
        v_o[:, lo:mid] = v[:, h * MLA_V:(h + 1) * MLA_V].astype(v_o.dtype)
        v_o[:, mid:hi] = ones


def _mla_proj(c_q, ckv, cos, sin, q_norm, kv_norm, wq, wk, wv):
    m = c_q.shape[0]
    tm = min(ROW_TILE, m)
    row = lambda n: pl.BlockSpec((tm, n), lambda i: (i, 0))
    width = MLA_HEADS * ATTN_HEAD_WIDTH
    return pl.pallas_call(
        _mla_proj_kernel, grid=(m // tm,),
        in_specs=[row(c_q.shape[1]), row(ckv.shape[1]), row(LANES), row(LANES),
                  _resident((1, Q_LORA)), _resident((1, KV_LORA)), _resident(wq.shape),
                  _resident(wk.shape), _resident(wv.shape)],
        out_specs=[row(width)] * 3,
        out_shape=[jax.ShapeDtypeStruct((m, width), BF16)] * 3,
        compiler_params=_params("parallel"), name="mla_proj",
    )(c_q, ckv, cos, sin, q_norm.reshape(1, Q_LORA), kv_norm.reshape(1, KV_LORA), wq, wk, wv)


def _attn_kernel(q_ref, k_ref, v_ref, o_ref, acc_sc, *, tile):
    qi = pl.program_id(1)
    hw = ATTN_HEAD_WIDTH
    heads = range(MLA_HEADS)
    row = lax.broadcasted_iota(jnp.int32, (tile, tile), 0)
    col = lax.broadcasted_iota(jnp.int32, (tile, tile), 1)
    acc_sc[...] = jnp.zeros_like(acc_sc)

    def block(start, width, m_prev, masked):
        rows = pl.ds(pl.multiple_of(start, tile), width)
        s, m_new, alpha, pv = [], [], [], []
        for h in heads:
            x = lax.dot_general(q_ref[:, h * hw:(h + 1) * hw], k_ref[rows, h * hw:(h + 1) * hw],
                                (((1,), (1,)), ((), ())), preferred_element_type=F32)
            if masked:
                x = jnp.where(row >= col, x, -jnp.inf)
            s.append(x)
            m_new.append(jnp.maximum(m_prev[h], jnp.max(x, axis=-1, keepdims=True)))
        for h in heads:
            p = jnp.exp2(s[h] - m_new[h]).astype(BF16)
            pv.append(jnp.dot(p, v_ref[rows, h * hw:(h + 1) * hw], preferred_element_type=F32))
            alpha.append(jnp.exp2(m_prev[h] - m_new[h]))
        for h in heads:
            acc_sc[h] = alpha[h] * acc_sc[h] + pv[h]
        return m_new

    m_run = [jnp.full((tile, 1), -jnp.inf, F32) for _ in heads]
    done = 0
    for mult in ATTN_BLOCK_MULTIPLES:
        count = (qi - done) // mult
        m_run = lax.fori_loop(
            0, count,
            lambda j, m, done=done, mult=mult: block((done + j * mult) * tile, mult * tile, m, False), m_run)
        done = done + count * mult
    block(qi * tile, tile, m_run, True)
    for h in heads:
        acc = acc_sc[h]
        o_ref[:, h * MLA_V:(h + 1) * MLA_V] = (acc[:, :MLA_V] / acc[:, LANES:LANES + MLA_V]).astype(o_ref.dtype)


def _attention(q, k, v, batch, seq):
    tile = min(ATTN_TILE, seq)
    width = q.shape[-1]
    r3 = lambda a: a.reshape(batch, seq, width)
    kv_blk = pl.BlockSpec((None, seq, width), lambda b, t: (b, 0, 0))
    out = pl.pallas_call(
        functools.partial(_attn_kernel, tile=tile),
        grid=(batch, seq // tile),
        in_specs=[pl.BlockSpec((None, tile, width), lambda b, t: (b, t, 0)), kv_blk, kv_blk],
        out_specs=pl.BlockSpec((None, tile, MLA_HEADS * MLA_V), lambda b, t: (b, t, 0)),
        out_shape=jax.ShapeDtypeStruct((batch, seq, MLA_HEADS * MLA_V), BF16),
        scratch_shapes=[pltpu.VMEM((MLA_HEADS, tile, ATTN_HEAD_WIDTH), F32)],
        compiler_params=_params("parallel", "arbitrary"), name="mla_attention",
    )(r3(q), r3(k), r3(v))
    return out.reshape(batch * seq, MLA_HEADS * MLA_V)


def _gdn_in_proj_kernel(x_ref, nw_ref, wqkv_ref, wz_ref, wba_ref, cw_ref, qkv_o, z_o, ba_o, tail_sc,
                        *, tiles_per_seq):
    @pl.when(pl.program_id(0) % tiles_per_seq == 0)
    def _():
        tail_sc[...] = jnp.zeros_like(tail_sc)

    dk = GDN_HEAD_DIM
    tm = x_ref.shape[0]
    h = _rms(x_ref[...], nw_ref[...]).astype(BF16)
    slab = 2 * dk
    n_qk = 2 * GDN_K_HEADS * dk
    n_slabs = wqkv_ref.shape[1] // slab
    z_every = n_slabs * slab // wz_ref.shape[1]
    for j in range(n_slabs):
        c0 = j * slab
        cols = slice(c0, c0 + slab)
        tail = tail_sc[:, cols]
        for r0 in range(0, tm, GDN_PROJ_SUBTILE):
            rows = slice(r0, r0 + GDN_PROJ_SUBTILE)
            y = jnp.dot(h[rows], wqkv_ref[:, cols], preferred_element_type=F32)
            act = _silu(_causal_conv_vpu(y, tail, cw_ref[:, cols]))
            tail = y[GDN_PROJ_SUBTILE - SUBLANES:]
            for c1 in range(0, slab, dk):
                a = act[:, c1:c1 + dk]
                if c0 < n_qk:
                    scale = dk ** -0.5 if c0 < n_qk // 2 else 1.0
                    a = a * (lax.rsqrt(jnp.sum(a * a, axis=-1, keepdims=True) + NORM_EPS) * scale)
                qkv_o[rows, c0 + c1:c0 + c1 + dk] = a.astype(qkv_o.dtype)
        tail_sc[:, cols] = tail
        if j % z_every == z_every - 1:
            zc = slice((j // z_every) * slab, (j // z_every + 1) * slab)
            z_o[:, zc] = jnp.dot(h, wz_ref[:, zc], preferred_element_type=F32).astype(z_o.dtype)
    ba_o[...] = jnp.dot(h, wba_ref[...], preferred_element_type=F32)


def _gdn_in_proj(x, nw, w_qkv, w_z, w_ba, conv_w, seq):
    m, d = x.shape
    tm = min(ROW_TILE, seq)
    row = lambda n: pl.BlockSpec((tm, n), lambda i: (i, 0))
    weights = [w_qkv, w_z, w_ba]
    widths = [_weight_cols(w) for w in weights]
    return pl.pallas_call(
        functools.partial(_gdn_in_proj_kernel, tiles_per_seq=seq // tm),
        grid=(m // tm,),
        in_specs=[row(d), _resident((1, d)), *[_weight_spec(w) for w in weights], _resident(conv_w.shape)],
        out_specs=[row(n) for n in widths],
        out_shape=[jax.ShapeDtypeStruct((m, n), dt) for n, dt in zip(widths, (BF16, BF16, F32))],
        scratch_shapes=[pltpu.VMEM((SUBLANES, widths[0]), F32)],
        compiler_params=_params("arbitrary"), name="gdn_in_proj",
    )(x, nw.reshape(1, d), *[_weight_array(w) for w in weights], conv_w)


def _nilpotent_inverse_batch(mats, order, eye):
    xs = [eye - m for m in mats]
    ps = mats
    k = 1
    while 2 * k < order:
        ps = [_mm(p, p) for p in ps]
        xs = [x + _mm(x, p) for x, p in zip(xs, ps)]
        k *= 2
    return xs


def _unit_lower_inverse_batch(mats):
    c = mats[0].shape[0]
    r = lax.broadcasted_iota(jnp.int32, (c, c), 0)
    q = lax.broadcasted_iota(jnp.int32, (c, c), 1)
    eye = (r == q).astype(F32)
    same_block = (r // GDN_SOLVE_BLOCK) == (q // GDN_SOLVE_BLOCK)
    ds = [jnp.where(same_block, a, 0.0) for a in mats]
    lows = [a - d for a, d in zip(mats, ds)]
    xs = _nilpotent_inverse_batch(ds, GDN_SOLVE_BLOCK, eye)
    ns = [_mm(x, low) for x, low in zip(xs, lows)]
    ys = _nilpotent_inverse_batch(ns, c // GDN_SOLVE_BLOCK, eye)
    return [_mm(y, x) for y, x in zip(ys, xs)]


def _gdn_kernel(qkv_ref, z_ref, ba_ref, gp_ref, nw_ref, o_ref, s_sc, u_sc, wq_sc, kd_sc, qk_sc):
    @pl.when(pl.program_id(1) == 0)
    def _():
        s_sc[...] = jnp.zeros_like(s_sc)

    dk = GDN_HEAD_DIM
    n_kh, n_vh = GDN_K_HEADS, GDN_V_HEADS
    rep = n_vh // n_kh
    c = GDN_CHUNK_ROWS
    t = qkv_ref.shape[0]
    n_chunks = t // c

    def head_cols(j):
        return slice(j * dk, (j + 1) * dk)

    ba = ba_ref[...]
    lane = lax.broadcasted_iota(jnp.int32, ba.shape, 1)
    gates = jnp.where(lane >= n_vh, -jnp.exp(gp_ref[0:1]) * _softplus(ba + gp_ref[1:2]),
                      _sigmoid(ba))

    ri = lax.broadcasted_iota(jnp.int32, (c, c), 0)
    ci = lax.broadcasted_iota(jnp.int32, (c, c), 1)
    incl = (ci <= ri).astype(F32)
    incl_t = (ri <= ci).astype(F32)
    lower = ri >= ci
    strict = ri > ci

    rows_of = [slice(n * c, (n + 1) * c) for n in range(n_chunks)]
    gch = [gates[r] for r in rows_of]
    gc = [_mm_f32(incl, g) for g in gch]
    gc_t = [lax.dot_general(g, incl_t, (((0,), (0,)), ((), ())), preferred_element_type=F32,
                            precision=lax.Precision.HIGHEST) for g in gch]
    chunk_decay = {(n, h): jnp.exp(gc[n][c - 1:c, n_vh + h:n_vh + h + 1])
                   for n in range(n_chunks) for h in range(n_vh)}

    systems = [(n, h) for n in range(n_chunks) for h in range(n_vh)]
    for s0 in range(0, len(systems), GDN_SYSTEM_BATCH):
        batch = systems[s0:s0 + GDN_SYSTEM_BATCH]
        pairs = sorted({(n, h // rep) for n, h in batch})
        gcol = {(n, h): gc[n][:, n_vh + h:n_vh + h + 1] for n, h in batch}
        beta = {(n, h): gch[n][:, h:h + 1] for n, h in batch}
        decay = {}
        for n, h in batch:
            diff = gcol[n, h] - gc_t[n][n_vh + h:n_vh + h + 1, :]
            decay[n, h] = jnp.where(lower, jnp.exp(jnp.where(lower, diff, 0.0)), 0.0)
        q16 = {(n, kh): qkv_ref[rows_of[n], head_cols(kh)] for n, kh in pairs}
        k16 = {(n, kh): qkv_ref[rows_of[n], head_cols(n_kh + kh)] for n, kh in pairs}
        k32 = {p: k16[p].astype(F32) for p in pairs}
        kb = {(n, h): k32[n, h // rep] * beta[n, h] for n, h in batch}
        kk = {(n, h): _mm_nt(kb[n, h], k16[n, h // rep]) for n, h in batch}
        qk = {p: _mm_nt(q16[p], k16[p]) for p in pairs}
        t_inv = _unit_lower_inverse_batch([jnp.where(strict, kk[s] * decay[s], 0.0) for s in batch])
        egc = {s: jnp.exp(gcol[s]) for s in batch}
        uw = [_mm(ti, jnp.concatenate(
            [qkv_ref[rows_of[n], head_cols(2 * n_kh + h)].astype(F32) * beta[n, h], kb[n, h] * egc[n, h]], axis=1))
            for ti, (n, h) in zip(t_inv, batch)]
        for (n, h), uw_s in zip(batch, uw):
            rows = rows_of[n]
            u_sc[h, rows] = uw_s[:, :dk]
            wq_sc[h, n, :c] = uw_s[:, dk:].astype(BF16)
            wq_sc[h, n, c:] = (q16[n, h // rep].astype(F32) * egc[n, h]).astype(BF16)
            gl = gc[n][c - 1:c, n_vh + h:n_vh + h + 1]
            kd_sc[h, rows] = (k32[n, h // rep] * jnp.exp(gl - gcol[n, h])).astype(BF16)
            qk_sc[h, rows] = (qk[n, h // rep] * decay[n, h]).astype(BF16)

    states = [s_sc[h] for h in range(n_vh)]
    for n in range(n_chunks):
        rows = slice(n * c, (n + 1) * c)
        ws_qs = [jnp.dot(wq_sc[h, n], states[h].astype(BF16), preferred_element_type=F32) for h in range(n_vh)]
        v_new = [(u_sc[h, rows] - ws_qs[h][:c]).astype(BF16) for h in range(n_vh)]
        outs = [ws_qs[h][c:] + jnp.dot(qk_sc[h, rows], v_new[h], preferred_element_type=F32)
                for h in range(n_vh)]
        states = [states[h] * chunk_decay[n, h]
                  + lax.dot_general(kd_sc[h, rows], v_new[h], (((0,), (0,)), ((), ())),
                                    preferred_element_type=F32) for h in range(n_vh)]
        for h in range(n_vh):
            cols = slice(h * dk, (h + 1) * dk)
            o = _rms(outs[h], nw_ref[...]) * _silu(z_ref[rows, cols].astype(F32))
            o_ref[rows, cols] = o.astype(o_ref.dtype)
    for h in range(n_vh):
        s_sc[h] = states[h]


def _gdn_core(qkv, z, ba, a_log, dt_bias, norm_w, batch, seq):
    dk = GDN_HEAD_DIM
    n_kh, n_vh = GDN_K_HEADS, GDN_V_HEADS
    tt = min(GDN_TIME_TILE, seq)
    c = GDN_CHUNK_ROWS
    qkv = qkv.reshape(batch, seq, qkv.shape[-1])
    z = z.reshape(batch, seq, z.shape[-1])
    ba = ba.reshape(batch, seq, ba.shape[-1])
    pad = jnp.zeros((n_vh,), F32)
    gate_params = jnp.stack([jnp.concatenate([pad, a_log]), jnp.concatenate([pad, dt_bias])])
    blk = lambda n: pl.BlockSpec((None, tt, n), lambda b, t: (b, t, 0))
    out = pl.pallas_call(
        _gdn_kernel, grid=(batch, seq // tt),
        in_specs=[blk(qkv.shape[-1]), blk(z.shape[-1]), blk(ba.shape[-1]),
                  _resident(gate_params.shape), _resident((1, dk))],
        out_specs=blk(n_vh * dk),
        out_shape=jax.ShapeDtypeStruct((batch, seq, n_vh * dk), BF16),
        scratch_shapes=[pltpu.VMEM((n_vh, dk, dk), F32),
                        pltpu.VMEM((n_vh, tt, dk), F32),
                        pltpu.VMEM((n_vh, tt // c, 2 * c, dk), BF16),
                        pltpu.VMEM((n_vh, tt, dk), BF16),
                        pltpu.VMEM((n_vh, tt, c), BF16)],
        compiler_params=_params("parallel", "arbitrary"), name="gdn_core",
    )(qkv, z, ba, gate_params, norm_w.reshape(1, dk))
    return out.reshape(batch * seq, n_vh * dk)


def _hybrid_layer(x, cos, sin, npre, npost, w_in_all, i, conv_w, conv_b, gate_a_w, gate_a_b, gate_x_w, gate_x_b,
                  lam, q_norm, w_uq, kv_norm, w_ukv, w_out_all, batch, seq, x_is_ours):
    rg_w = lam.shape[0]
    d_model = w_in_all.shape[1]
    o3 = 2 * rg_w + Q_LORA
    assert (2 * rg_w) % Q_LORA == 0 and w_out_all.shape[1] == 2 * rg_w
    w_ckv = jnp.concatenate([w_in_all[i][:, o3:], jnp.zeros((d_model, LANES - MLA_ROPE), BF16)], axis=1)
    weights = [_Window(w_in_all, (None, d_model, rg_w), (i, 0, 0)),
               _Window(w_in_all, (None, d_model, rg_w), (i, 0, 1)),
               _Window(w_in_all, (None, d_model, Q_LORA), (i, 0, 2 * rg_w // Q_LORA)), w_ckv]
    x_r, gate_r, c_q, ckv = _norm_proj(x, npre, weights, [BF16] * 4)

    y_a = _rglru(x_r, gate_r, conv_w, conv_b, gate_a_w.astype(BF16), gate_a_b, gate_x_w.astype(BF16),
                 gate_x_b, lam, batch, seq)

    wq = w_uq.reshape(Q_LORA, MLA_HEADS, MLA_QK)
    wq = jnp.concatenate([wq, jnp.zeros((Q_LORA, MLA_HEADS, ATTN_HEAD_WIDTH - MLA_QK), wq.dtype)], axis=2)
    wq = wq.reshape(Q_LORA, MLA_HEADS * ATTN_HEAD_WIDTH)
    wkv = w_ukv.reshape(KV_LORA, MLA_HEADS, MLA_NOPE + MLA_V)
    wk = wkv[:, :, :MLA_NOPE].reshape(KV_LORA, MLA_HEADS * MLA_NOPE)
    wv = wkv[:, :, MLA_NOPE:].reshape(KV_LORA, MLA_HEADS * MLA_V)
    q, k, v = _mla_proj(c_q, ckv, cos, sin, q_norm, kv_norm, wq.astype(BF16), wk.astype(BF16),
                        wv.astype(BF16))
    y_b = _attention(q, k, v, batch, seq)

    d_out = w_out_all.shape[2]
    w_out = [_Window(w_out_all, (None, rg_w, d_out), (i, 0, 0)), _Window(w_out_all, (None, rg_w, d_out), (i, 1, 0))]
    return _out_proj([y_a, y_b], w_out, x, npost, in_place=x_is_ours)


def _gdn_layer(x, npre, npost, w_in_all, i, conv_w, a_log, dt_bias, norm_w, w_out_all, batch, seq):
    d_model = w_in_all.shape[1]
    conv_c = conv_w.shape[1]
    v_w = GDN_V_HEADS * GDN_HEAD_DIM
    assert conv_c % v_w == 0
    qkv, z, ba = _gdn_in_proj(x, npre, _Window(w_in_all, (None, d_model, conv_c), (i, 0, 0)),
                              _Window(w_in_all, (None, d_model, v_w), (i, 0, conv_c // v_w)),
                              w_in_all[i][:, conv_c + v_w:], conv_w, seq)
    y = _gdn_core(qkv, z, ba, a_log, dt_bias, norm_w, batch, seq)
    return _out_proj([y], [_layer_window(w_out_all, i)], x, npost)


def kernel(x, positions, norm_mix_pre, norm_mix_post, norm_ffn_pre, norm_ffn_post,
           hy_w_in, rg_conv_w, rg_conv_b, rg_gate_a_w, rg_gate_a_b, rg_gate_x_w,
           rg_gate_x_b, rg_lambda, mla_q_norm, mla_w_uq, mla_kv_norm, mla_w_ukv, hy_w_out,
           gdn_w_in, gdn_conv_w, gdn_a_log, gdn_dt_bias, gdn_norm, gdn_w_out,
           ffn_w_gate, ffn_w_up, ffn_w_down):
    batch, seq, d = x.shape
    depth = norm_mix_pre.shape[0]
    cos, sin = _rope_tables(positions)
    wg_all, wu_all, wd_all, hy_in_all, hy_out_all, gdn_in_all, gdn_out_all = (
        w.astype(BF16) for w in (ffn_w_gate, ffn_w_up, ffn_w_down, hy_w_in, hy_w_out, gdn_w_in, gdn_w_out))
    x = x.reshape(batch * seq, d)
    for layer in range(depth):
        i = layer // 2
        if layer % 2 == 0:
            x = _hybrid_layer(x, cos, sin, norm_mix_pre[layer], norm_mix_post[layer], hy_in_all, i,
                              rg_conv_w[i], rg_conv_b[i], rg_gate_a_w[i], rg_gate_a_b[i], rg_gate_x_w[i],
                              rg_gate_x_b[i], rg_lambda[i], mla_q_norm[i], mla_w_uq[i], mla_kv_norm[i],
                              mla_w_ukv[i], hy_out_all, batch, seq, x_is_ours=layer > 0)
        else:
            x = _gdn_layer(x, norm_mix_pre[layer], norm_mix_post[layer], gdn_in_all, i, gdn_conv_w[i],
                           gdn_a_log[i], gdn_dt_bias[i], gdn_norm[i], gdn_out_all, batch, seq)
        x = _ffn(x, norm_ffn_pre[layer], wg_all, wu_all, wd_all, norm_ffn_post[layer], layer)
    return x.reshape(batch, seq, d)
```

```python
import functools
from typing import NamedTuple

import jax
import jax.numpy as jnp
from jax import lax
from jax.experimental import pallas as pl
from jax.experimental.pallas import tpu as pltpu

F32 = jnp.float32
BF16 = jnp.bfloat16

NORM_EPS = 1e-6
LANES = 128
SUBLANES = 8
VMEM_LIMIT_BYTES = 48 * 1024 * 1024

CONV_WIDTH = 4
RG_BLOCKS = 8
RG_C = 8.0
MLA_HEADS = 8
MLA_NOPE = 128
MLA_ROPE = 64
MLA_V = 128
MLA_QK = MLA_NOPE + MLA_ROPE
Q_LORA = 512
KV_LORA = 256
ROPE_THETA = 10000.0
GDN_K_HEADS = 8
GDN_V_HEADS = 16
GDN_HEAD_DIM = 128
GDN_CHUNK_ROWS = 128
GDN_SOLVE_BLOCK = 16
GDN_PROJ_SUBTILE = 128
GDN_SYSTEM_BATCH = 16

ROW_TILE = 1024
HYBRID_PROJ_ROW_TILE = 512
FFN_ROW_TILE = 1024
FFN_HIDDEN_TILE = 256
ATTN_TILE = 256
ATTN_BLOCK_MULTIPLES = (4, 2, 1)
RG_TIME_TILE = 256
GDN_TIME_TILE = 256


def _params(*semantics):
    return pltpu.CompilerParams(dimension_semantics=semantics, vmem_limit_bytes=VMEM_LIMIT_BYTES)


def _rms(x, w):
    return x * lax.rsqrt(jnp.mean(x * x, axis=-1, keepdims=True) + NORM_EPS) * w


def _mm(a, b):
    return jnp.dot(a.astype(BF16), b.astype(BF16), preferred_element_type=F32)


def _mm_f32(a, b):
    return jnp.dot(a, b, preferred_element_type=F32, precision=lax.Precision.HIGHEST)


def _mm_nt(a, b):
    return lax.dot_general(a.astype(BF16), b.astype(BF16), (((1,), (1,)), ((), ())),
                           preferred_element_type=F32)


def _mm_tn(a, b):
    return lax.dot_general(a.astype(BF16), b.astype(BF16), (((0,), (0,)), ((), ())),
                           preferred_element_type=F32)


def _resident(shape):
    zeros = (0,) * len(shape)
    return pl.BlockSpec(shape, lambda *_: zeros, pipeline_mode=pl.Buffered(1))


class _Window(NamedTuple):
    array: jax.Array
    shape: tuple
    index: tuple


def _weight_spec(w):
    if isinstance(w, _Window):
        return pl.BlockSpec(w.shape, lambda *_: w.index, pipeline_mode=pl.Buffered(1))
    return _resident(w.shape)


def _weight_array(w):
    return w.array if isinstance(w, _Window) else w


def _weight_cols(w):
    return w.shape[-1]


def _shift_matrices(t):
    r = lax.broadcasted_iota(jnp.int32, (t, t), 0)
    c = lax.broadcasted_iota(jnp.int32, (t, t), 1)
    return [(r - c == s).astype(BF16) for s in range(1, CONV_WIDTH)]


def _causal_conv(x_bf16, prev_tail, w, shifts):
    x = x_bf16.astype(F32)
    y = x * w[CONV_WIDTH - 1:CONV_WIDTH]
    row = lax.broadcasted_iota(jnp.int32, prev_tail.shape, 0)
    head_fix = jnp.zeros(prev_tail.shape, F32)
    for s, shift in zip(range(1, CONV_WIDTH), shifts):
        w_s = w[CONV_WIDTH - 1 - s:CONV_WIDTH - s]
        y = y + jnp.dot(shift, x_bf16, preferred_element_type=F32) * w_s
        head_fix = head_fix + jnp.where(row < s, pltpu.roll(prev_tail, s, axis=0), 0.0) * w_s
    return jnp.concatenate([y[:SUBLANES] + head_fix, y[SUBLANES:]], axis=0), x


def _causal_conv_vpu(x, prev_tail, w):
    assert CONV_WIDTH == 4
    row = lax.broadcasted_iota(jnp.int32, prev_tail.shape, 0)

    def shift(v, v_tail, s):
        vs = pltpu.roll(v, s, axis=0)
        first = jnp.where(row < s, pltpu.roll(v_tail, s, axis=0), vs[:SUBLANES])
        return jnp.concatenate([first, vs[SUBLANES:]], axis=0)

    w0, w1, w2, w3 = (w[i:i + 1] for i in range(CONV_WIDTH))
    x1 = shift(x, prev_tail, 1)
    p = w1 * x + w0 * x1
    p_tail = w1 * prev_tail + w0 * pltpu.roll(prev_tail, 1, axis=0)
    return w3 * x + w2 * x1 + shift(p, p_tail, 2)


def _sigmoid(x):
    return 0.5 + 0.5 * jnp.tanh(0.5 * x)


def _silu(x):
    h = 0.5 * x
    return h + h * jnp.tanh(h)


def _softplus(x):
    return jnp.maximum(x, 0.0) + jnp.log1p(jnp.exp(-jnp.abs(x)))


def _out_proj_kernel(*refs, n_in):
    a_refs, w_refs = refs[:n_in], refs[n_in:2 * n_in]
    x_ref, nw_ref, o_ref = refs[2 * n_in:]
    acc = jnp.dot(a_refs[0][...], w_refs[0][...], preferred_element_type=F32)
    for a_ref, w_ref in zip(a_refs[1:], w_refs[1:]):
        acc = acc + jnp.dot(a_ref[...], w_ref[...], preferred_element_type=F32)
    o_ref[...] = x_ref[...] + _rms(acc, nw_ref[...])


def _out_proj(acts, weights, x, nw, in_place=True):
    m, d = x.shape
    tm = min(ROW_TILE, m)
    n_in = len(acts)
    in_specs = [pl.BlockSpec((tm, a.shape[1]), lambda i: (i, 0)) for a in acts]
    in_specs += [_weight_spec(w) for w in weights]
    in_specs += [pl.BlockSpec((tm, d), lambda i: (i, 0)), _resident((1, d))]
    return pl.pallas_call(
        functools.partial(_out_proj_kernel, n_in=n_in),
        grid=(m // tm,), in_specs=in_specs,
        out_specs=pl.BlockSpec((tm, d), lambda i: (i, 0)),
        out_shape=jax.ShapeDtypeStruct((m, d), F32),
        input_output_aliases={2 * n_in: 0} if in_place else {},
        compiler_params=_params("parallel"), name="out_proj",
    )(*acts, *[_weight_array(w) for w in weights], x, nw.reshape(1, d))


def _ffn_kernel(x_ref, npre_ref, wg_ref, wu_ref, wd_ref, npost_ref, o_ref, *, hidden_tile):
    x = x_ref[...]
    h = _rms(x, npre_ref[...]).astype(BF16)
    hidden = wg_ref.shape[1]
    acc = None
    for c0 in range(0, hidden, hidden_tile):
        g = jnp.dot(h, wg_ref[:, c0:c0 + hidden_tile], preferred_element_type=F32)
        u = jnp.dot(h, wu_ref[:, c0:c0 + hidden_tile], preferred_element_type=F32)
        a = (_silu(g) * u).astype(BF16)
        part = jnp.dot(a, wd_ref[c0:c0 + hidden_tile, :], preferred_element_type=F32)
        acc = part if acc is None else acc + part
    o_ref[...] = x + _rms(acc, npost_ref[...])


def _layer_window(stacked, layer):
    return _Window(stacked, (None,) + stacked.shape[1:], (layer, 0, 0))


def _ffn(x, npre, wg_all, wu_all, wd_all, npost, layer):
    m, d = x.shape
    tm = min(FFN_ROW_TILE, m)
    hidden = wg_all.shape[2]
    windows = [_layer_window(w, layer) for w in (wg_all, wu_all, wd_all)]
    hidden_tile = FFN_HIDDEN_TILE if hidden % FFN_HIDDEN_TILE == 0 else hidden
    return pl.pallas_call(
        functools.partial(_ffn_kernel, hidden_tile=hidden_tile),
        grid=(m // tm,),
        in_specs=[pl.BlockSpec((tm, d), lambda i: (i, 0)), _resident((1, d)),
                  *[_weight_spec(w) for w in windows], _resident((1, d))],
        out_specs=pl.BlockSpec((tm, d), lambda i: (i, 0)),
        out_shape=jax.ShapeDtypeStruct((m, d), F32),
        input_output_aliases={0: 0},
        compiler_params=_params("parallel"), name="ffn",
    )(x, npre.reshape(1, d), wg_all, wu_all, wd_all, npost.reshape(1, d))


def _rope_table_kernel(pos_ref, invf_ref, cos_ref, sin_ref):
    ang = pos_ref[...].astype(F32) * invf_ref[...]
    cos_ref[...] = jnp.cos(ang)
    sin_ref[...] = jnp.sin(ang)


def _rope_tables(positions):
    m = positions.size
    tm = min(ROW_TILE, m)
    half = MLA_ROPE // 2
    inv_freq = 1.0 / (ROPE_THETA ** (jnp.arange(0, MLA_ROPE, 2, dtype=F32) / MLA_ROPE))
    invf = jnp.tile(inv_freq, LANES // half).reshape(1, LANES)
    return pl.pallas_call(
        _rope_table_kernel, grid=(m // tm,),
        in_specs=[pl.BlockSpec((tm, 1), lambda i: (i, 0)), _resident((1, LANES))],
        out_specs=[pl.BlockSpec((tm, LANES), lambda i: (i, 0))] * 2,
        out_shape=[jax.ShapeDtypeStruct((m, LANES), F32)] * 2,
        compiler_params=_params("parallel"), name="rope_tables",
    )(positions.reshape(m, 1), invf)


def _rglru_kernel(xr_ref, gr_ref, cw_ref, cb_ref, wa_ref, ba_ref, wx_ref, bx_ref, lam_ref, o_ref,
                  h_sc, tail_sc):
    @pl.when(pl.program_id(1) == 0)
    def _():
        h_sc[...] = jnp.zeros_like(h_sc)
        tail_sc[...] = jnp.zeros_like(tail_sc)

    t, width = xr_ref.shape
    xc, x = _causal_conv(xr_ref[...], tail_sc[...], cw_ref[...], _shift_matrices(t))
    xc = xc + cb_ref[...]
    tail_sc[...] = x[t - SUBLANES:]

    xcb = xc.astype(BF16)
    bw = width // RG_BLOCKS
    ra = jnp.concatenate([jnp.dot(xcb[:, n * bw:(n + 1) * bw], wa_ref[n], preferred_element_type=F32)
                          for n in range(RG_BLOCKS)], axis=1)
    rx = jnp.concatenate([jnp.dot(xcb[:, n * bw:(n + 1) * bw], wx_ref[n], preferred_element_type=F32)
                          for n in range(RG_BLOCKS)], axis=1)
    r = _sigmoid(ra + ba_ref[...])
    i = _sigmoid(rx + bx_ref[...])
    log_a = (-RG_C * _softplus(-lam_ref[...])) * r
    a = jnp.exp(log_a)
    th = jnp.tanh(log_a)
    b = jnp.sqrt(-2.0 * th / (1.0 - th)) * (i * xc)

    groups = t // SUBLANES
    a = a.reshape(groups, SUBLANES, width)
    b = b.reshape(groups, SUBLANES, width)
    sub = lax.broadcasted_iota(jnp.int32, (groups, SUBLANES, width), 1)
    s = 1
    while s < SUBLANES:
        keep = sub >= s
        b = jnp.where(keep, a * pltpu.roll(b, s, axis=1) + b, b)
        a = jnp.where(keep, a * pltpu.roll(a, s, axis=1), a)
        s *= 2
    carry = h_sc[0:1]
    hs = []
    for g in range(groups):
        hs.append(b[g] + a[g] * carry)
        carry = hs[-1][SUBLANES - 1:]
    h_sc[...] = jnp.broadcast_to(carry, h_sc.shape)
    h = jnp.concatenate(hs, axis=0)
    o_ref[...] = (h * jax.nn.gelu(gr_ref[...].astype(F32))).astype(o_ref.dtype)


def _rglru(x_r, gate_r, conv_w, conv_b, wa, ba, wx, bx, lam, batch, seq):
    width = x_r.shape[-1]
    tt = min(RG_TIME_TILE, seq)
    x_r = x_r.reshape(batch, seq, width)
    gate_r = gate_r.reshape(batch, seq, width)
    blk = pl.BlockSpec((None, tt, width), lambda b, t: (b, t, 0))
    vec = _resident((1, width))
    out = pl.pallas_call(
        _rglru_kernel, grid=(batch, seq // tt),
        in_specs=[blk, blk, _resident(conv_w.shape), vec, _resident(wa.shape), vec,
                  _resident(wx.shape), vec, vec],
        out_specs=blk,
        out_shape=jax.ShapeDtypeStruct((batch, seq, width), BF16),
        scratch_shapes=[pltpu.VMEM((SUBLANES, width), F32), pltpu.VMEM((SUBLANES, width), F32)],
        compiler_params=_params("parallel", "arbitrary"), name="rglru",
    )(x_r, gate_r, conv_w, conv_b.reshape(1, width), wa, ba.reshape(1, width), wx,
      bx.reshape(1, width), lam.reshape(1, width))
    return out.reshape(batch * seq, width)


ATTN_HEAD_WIDTH = 2 * LANES
LOG2_E = 1.4426950408889634


def _hybrid_in_proj_kernel(x_ref, npre_ref, wxr_ref, wgr_ref, wcq_ref, wckv_ref, cos_ref, sin_ref,
                           qnw_ref, kvnw_ref, wq_ref, wk_ref, wv_ref, xr_o, gr_o, q_o, k_o, v_o):
    tm = x_ref.shape[0]
    h = _rms(x_ref[...], npre_ref[...]).astype(BF16)
    xr_o[...] = jnp.dot(h, wxr_ref[...], preferred_element_type=F32).astype(xr_o.dtype)
    gr_o[...] = jnp.dot(h, wgr_ref[...], preferred_element_type=F32).astype(gr_o.dtype)
    c_q = jnp.dot(h, wcq_ref[...], preferred_element_type=F32)
    ckv = jnp.dot(h, wckv_ref[...], preferred_element_type=F32)
    hw = ATTN_HEAD_WIDTH
    half = MLA_ROPE // 2
    lane = lax.broadcasted_iota(jnp.int32, (tm, LANES), 1)
    cos = jnp.where(lane < MLA_ROPE, cos_ref[...], 0.0)
    sin = jnp.where(lane < MLA_ROPE, sin_ref[...], 0.0)

    def rope(x):
        rot = jnp.where(lane < half, -pltpu.roll(x, LANES - half, axis=1), pltpu.roll(x, half, axis=1))
        return x * cos + rot * sin

    q_scale = MLA_QK ** -0.5 * LOG2_E
    qn = _rms(c_q, qnw_ref[...]).astype(BF16)
    q = jnp.dot(qn, wq_ref[...], preferred_element_type=F32) * q_scale
    kvn = _rms(ckv[:, :KV_LORA], kvnw_ref[...]).astype(BF16)
    k_nope = jnp.dot(kvn, wk_ref[...], preferred_element_type=F32)
    v = jnp.dot(kvn, wv_ref[...], preferred_element_type=F32)
    k_rope = rope(ckv[:, KV_LORA:]).astype(k_o.dtype)
    ones = jnp.ones((tm, LANES), v_o.dtype)
    for h in range(MLA_HEADS):
        lo, mid, hi = h * hw, h * hw + LANES, (h + 1) * hw
        q_o[:, lo:mid] = q[:, lo:mid].astype(q_o.dtype)
        q_o[:, mid:hi] = rope(q[:, mid:hi]).astype(q_o.dtype)
        k_o[:, lo:mid] = k_nope[:, h * MLA_NOPE:(h + 1) * MLA_NOPE].astype(k_o.dtype)
        k_o[:, mid:hi] = k_rope
        v_o[:, lo:mid] = v[:, h * MLA_V:(h + 1) * MLA_V].astype(v_o.dtype)
        v_o[:, mid:hi] = ones


def _hybrid_in_proj(x, npre, in_weights, cos, sin, q_norm, kv_norm, wq, wk, wv):
    m, d = x.shape
    tm = min(HYBRID_PROJ_ROW_TILE, m)
    row = lambda n: pl.BlockSpec((tm, n), lambda i: (i, 0))
    rg_w = _weight_cols(in_weights[0])
    width = MLA_HEADS * ATTN_HEAD_WIDTH
    out_widths = [rg_w, rg_w, width, width, width]
    return pl.pallas_call(
        _hybrid_in_proj_kernel, grid=(m // tm,),
        in_specs=[row(d), _resident((1, d)), *[_weight_spec(w) for w in in_weights], row(LANES), row(LANES),
                  _resident((1, Q_LORA)), _resident((1, KV_LORA)), _resident(wq.shape),
                  _resident(wk.shape), _resident(wv.shape)],
        out_specs=[row(n) for n in out_widths],
        out_shape=[jax.ShapeDtypeStruct((m, n), BF16) for n in out_widths],
        compiler_params=_params("parallel"), name="hybrid_in_proj",
    )(x, npre.reshape(1, d), *[_weight_array(w) for w in in_weights], cos, sin,
      q_norm.reshape(1, Q_LORA), kv_norm.reshape(1, KV_LORA), wq, wk, wv)


def _attn_kernel(q_ref, k_ref, v_ref, o_ref, acc_sc, *, tile):
    qi = pl.program_id(1)
    hw = ATTN_HEAD_WIDTH
    heads = range(MLA_HEADS)
    row = lax.broadcasted_iota(jnp.int32, (tile, tile), 0)
    col = lax.broadcasted_iota(jnp.int32, (tile, tile), 1)
    acc_sc[...] = jnp.zeros_like(acc_sc)

    def block(start, width, m_prev, masked):
        rows = pl.ds(pl.multiple_of(start, tile), width)
        s, m_new, alpha, pv = [], [], [], []
        for h in heads:
            x = lax.dot_general(q_ref[:, h * hw:(h + 1) * hw], k_ref[rows, h * hw:(h + 1) * hw],
                                (((1,), (1,)), ((), ())), preferred_element_type=F32)
            if masked:
                x = jnp.where(row >= col, x, -jnp.inf)
            s.append(x)
            m_new.append(jnp.maximum(m_prev[h], jnp.max(x, axis=-1, keepdims=True)))
        for h in heads:
            p = jnp.exp2(s[h] - m_new[h]).astype(BF16)
            pv.append(jnp.dot(p, v_ref[rows, h * hw:(h + 1) * hw], preferred_element_type=F32))
            alpha.append(jnp.exp2(m_prev[h] - m_new[h]))
        for h in heads:
            acc_sc[h] = alpha[h] * acc_sc[h] + pv[h]
        return m_new

    m_run = [jnp.full((tile, 1), -jnp.inf, F32) for _ in heads]
    done = 0
    for mult in ATTN_BLOCK_MULTIPLES:
        count = (qi - done) // mult
        m_run = lax.fori_loop(
            0, count,
            lambda j, m, done=done, mult=mult: block((done + j * mult) * tile, mult * tile, m, False), m_run)
        done = done + count * mult
    block(qi * tile, tile, m_run, True)
    for h in heads:
        acc = acc_sc[h]
        o_ref[:, h * MLA_V:(h + 1) * MLA_V] = (acc[:, :MLA_V] / acc[:, LANES:LANES + MLA_V]).astype(o_ref.dtype)


def _attention(q, k, v, batch, seq):
    tile = min(ATTN_TILE, seq)
    width = q.shape[-1]
    r3 = lambda a: a.reshape(batch, seq, width)
    kv_blk = pl.BlockSpec((None, seq, width), lambda b, t: (b, 0, 0))
    out = pl.pallas_call(
        functools.partial(_attn_kernel, tile=tile),
        grid=(batch, seq // tile),
        in_specs=[pl.BlockSpec((None, tile, width), lambda b, t: (b, t, 0)), kv_blk, kv_blk],
        out_specs=pl.BlockSpec((None, tile, MLA_HEADS * MLA_V), lambda b, t: (b, t, 0)),
        out_shape=jax.ShapeDtypeStruct((batch, seq, MLA_HEADS * MLA_V), BF16),
        scratch_shapes=[pltpu.VMEM((MLA_HEADS, tile, ATTN_HEAD_WIDTH), F32)],
        compiler_params=_params("parallel", "arbitrary"), name="mla_attention",
    )(r3(q), r3(k), r3(v))
    return out.reshape(batch * seq, MLA_HEADS * MLA_V)


def _gdn_in_proj_kernel(x_ref, nw_ref, wqkv_ref, wz_ref, wba_ref, cw_ref, qkv_o, z_o, ba_o, tail_sc,
                        *, tiles_per_seq):
    @pl.when(pl.program_id(0) % tiles_per_seq == 0)
    def _():
        tail_sc[...] = jnp.zeros_like(tail_sc)

    dk = GDN_HEAD_DIM
    tm = x_ref.shape[0]
    h = _rms(x_ref[...], nw_ref[...]).astype(BF16)
    slab = 2 * dk
    n_qk = 2 * GDN_K_HEADS * dk
    n_slabs = wqkv_ref.shape[1] // slab
    z_every = n_slabs * slab // wz_ref.shape[1]
    for j in range(n_slabs):
        c0 = j * slab
        cols = slice(c0, c0 + slab)
        tail = tail_sc[:, cols]
        for r0 in range(0, tm, GDN_PROJ_SUBTILE):
            rows = slice(r0, r0 + GDN_PROJ_SUBTILE)
            y = jnp.dot(h[rows], wqkv_ref[:, cols], preferred_element_type=F32)
            act = _silu(_causal_conv_vpu(y, tail, cw_ref[:, cols]))
            tail = y[GDN_PROJ_SUBTILE - SUBLANES:]
            for c1 in range(0, slab, dk):
                a = act[:, c1:c1 + dk]
                if c0 < n_qk:
                    scale = dk ** -0.5 if c0 < n_qk // 2 else 1.0
                    a = a * (lax.rsqrt(jnp.sum(a * a, axis=-1, keepdims=True) + NORM_EPS) * scale)
                qkv_o[rows, c0 + c1:c0 + c1 + dk] = a.astype(qkv_o.dtype)
        tail_sc[:, cols] = tail
        if j % z_every == z_every - 1:
            zc = slice((j // z_every) * slab, (j // z_every + 1) * slab)
            z_o[:, zc] = jnp.dot(h, wz_ref[:, zc], preferred_element_type=F32).astype(z_o.dtype)
    ba_o[...] = jnp.dot(h, wba_ref[...], preferred_element_type=F32)


def _gdn_in_proj(x, nw, w_qkv, w_z, w_ba, conv_w, seq):
    m, d = x.shape
    tm = min(ROW_TILE, seq)
    row = lambda n: pl.BlockSpec((tm, n), lambda i: (i, 0))
    weights = [w_qkv, w_z, w_ba]
    widths = [_weight_cols(w) for w in weights]
    return pl.pallas_call(
        functools.partial(_gdn_in_proj_kernel, tiles_per_seq=seq // tm),
        grid=(m // tm,),
        in_specs=[row(d), _resident((1, d)), *[_weight_spec(w) for w in weights], _resident(conv_w.shape)],
        out_specs=[row(n) for n in widths],
        out_shape=[jax.ShapeDtypeStruct((m, n), dt) for n, dt in zip(widths, (BF16, BF16, F32))],
        scratch_shapes=[pltpu.VMEM((SUBLANES, widths[0]), F32)],
        compiler_params=_params("arbitrary"), name="gdn_in_proj",
    )(x, nw.reshape(1, d), *[_weight_array(w) for w in weights], conv_w)


def _nilpotent_inverse_batch(mats, order, eye):
    xs = [eye - m for m in mats]
    ps = mats
    k = 1
    while 2 * k < order:
        ps = [_mm(p, p) for p in ps]
        xs = [x + _mm(x, p) for x, p in zip(xs, ps)]
        k *= 2
    return xs


def _unit_lower_inverse_batch(mats):
    c = mats[0].shape[0]
    r = lax.broadcasted_iota(jnp.int32, (c, c), 0)
    q = lax.broadcasted_iota(jnp.int32, (c, c), 1)
    eye = (r == q).astype(F32)
    same_block = (r // GDN_SOLVE_BLOCK) == (q // GDN_SOLVE_BLOCK)
    ds = [jnp.where(same_block, a, 0.0) for a in mats]
    lows = [a - d for a, d in zip(mats, ds)]
    xs = _nilpotent_inverse_batch(ds, GDN_SOLVE_BLOCK, eye)
    ns = [_mm(x, low) for x, low in zip(xs, lows)]
    ys = _nilpotent_inverse_batch(ns, c // GDN_SOLVE_BLOCK, eye)
    return [_mm(y, x) for y, x in zip(ys, xs)]


def _gdn_kernel(qkv_ref, z_ref, ba_ref, gp_ref, nw_ref, o_ref, s_sc, u_sc, wq_sc, kd_sc, qk_sc):
    @pl.when(pl.program_id(1) == 0)
    def _():
        s_sc[...] = jnp.zeros_like(s_sc)

    dk = GDN_HEAD_DIM
    n_kh, n_vh = GDN_K_HEADS, GDN_V_HEADS
    rep = n_vh // n_kh
    c = GDN_CHUNK_ROWS
    t = qkv_ref.shape[0]
    n_chunks = t // c

    def head_cols(j):
        return slice(j * dk, (j + 1) * dk)

    ba = ba_ref[...]
    lane = lax.broadcasted_iota(jnp.int32, ba.shape, 1)
    gates = jnp.where(lane >= n_vh, -jnp.exp(gp_ref[0:1]) * _softplus(ba + gp_ref[1:2]),
                      _sigmoid(ba))

    ri = lax.broadcasted_iota(jnp.int32, (c, c), 0)
    ci = lax.broadcasted_iota(jnp.int32, (c, c), 1)
    incl = (ci <= ri).astype(F32)
    incl_t = (ri <= ci).astype(F32)
    lower = ri >= ci
    strict = ri > ci

    rows_of = [slice(n * c, (n + 1) * c) for n in range(n_chunks)]
    gch = [gates[r] for r in rows_of]
    gc = [_mm_f32(incl, g) for g in gch]
    gc_t = [lax.dot_general(g, incl_t, (((0,), (0,)), ((), ())), preferred_element_type=F32,
                            precision=lax.Precision.HIGHEST) for g in gch]
    chunk_decay = {(n, h): jnp.exp(gc[n][c - 1:c, n_vh + h:n_vh + h + 1])
                   for n in range(n_chunks) for h in range(n_vh)}

    systems = [(n, h) for n in range(n_chunks) for h in range(n_vh)]
    for s0 in range(0, len(systems), GDN_SYSTEM_BATCH):
        batch = systems[s0:s0 + GDN_SYSTEM_BATCH]
        pairs = sorted({(n, h // rep) for n, h in batch})
        gcol = {(n, h): gc[n][:, n_vh + h:n_vh + h + 1] for n, h in batch}
        beta = {(n, h): gch[n][:, h:h + 1] for n, h in batch}
        decay = {}
        for n, h in batch:
            diff = gcol[n, h] - gc_t[n][n_vh + h:n_vh + h + 1, :]
            decay[n, h] = jnp.where(lower, jnp.exp(jnp.where(lower, diff, 0.0)), 0.0)
        q16 = {(n, kh): qkv_ref[rows_of[n], head_cols(kh)] for n, kh in pairs}
        k16 = {(n, kh): qkv_ref[rows_of[n], head_cols(n_kh + kh)] for n, kh in pairs}
        k32 = {p: k16[p].astype(F32) for p in pairs}
        kb = {(n, h): k32[n, h // rep] * beta[n, h] for n, h in batch}
        kk = {(n, h): _mm_nt(kb[n, h], k16[n, h // rep]) for n, h in batch}
        qk = {p: _mm_nt(q16[p], k16[p]) for p in pairs}
        t_inv = _unit_lower_inverse_batch([jnp.where(strict, kk[s] * decay[s], 0.0) for s in batch])
        egc = {s: jnp.exp(gcol[s]) for s in batch}
        uw = [_mm(ti, jnp.concatenate(
            [qkv_ref[rows_of[n], head_cols(2 * n_kh + h)].astype(F32) * beta[n, h], kb[n, h] * egc[n, h]], axis=1))
            for ti, (n, h) in zip(t_inv, batch)]
        for (n, h), uw_s in zip(batch, uw):
            rows = rows_of[n]
            u_sc[h, rows] = uw_s[:, :dk]
            wq_sc[h, n, :c] = uw_s[:, dk:].astype(BF16)
            wq_sc[h, n, c:] = (q16[n, h // rep].astype(F32) * egc[n, h]).astype(BF16)
            gl = gc[n][c - 1:c, n_vh + h:n_vh + h + 1]
            kd_sc[h, rows] = (k32[n, h // rep] * jnp.exp(gl - gcol[n, h])).astype(BF16)
            qk_sc[h, rows] = (qk[n, h // rep] * decay[n, h]).astype(BF16)

    states = [s_sc[h] for h in range(n_vh)]
    for n in range(n_chunks):
        rows = slice(n * c, (n + 1) * c)
        ws_qs = [jnp.dot(wq_sc[h, n], states[h].astype(BF16), preferred_element_type=F32) for h in range(n_vh)]
        v_new = [(u_sc[h, rows] - ws_qs[h][:c]).astype(BF16) for h in range(n_vh)]
        outs = [ws_qs[h][c:] + jnp.dot(qk_sc[h, rows], v_new[h], preferred_element_type=F32)
                for h in range(n_vh)]
        states = [states[h] * chunk_decay[n, h]
                  + lax.dot_general(kd_sc[h, rows], v_new[h], (((0,), (0,)), ((), ())),
                                    preferred_element_type=F32) for h in range(n_vh)]
        for h in range(n_vh):
            cols = slice(h * dk, (h + 1) * dk)
            o = _rms(outs[h], nw_ref[...]) * _silu(z_ref[rows, cols].astype(F32))
            o_ref[rows, cols] = o.astype(o_ref.dtype)
    for h in range(n_vh):
        s_sc[h] = states[h]


def _gdn_core(qkv, z, ba, a_log, dt_bias, norm_w, batch, seq):
    dk = GDN_HEAD_DIM
    n_kh, n_vh = GDN_K_HEADS, GDN_V_HEADS
    tt = min(GDN_TIME_TILE, seq)
    c = GDN_CHUNK_ROWS
    qkv = qkv.reshape(batch, seq, qkv.shape[-1])
    z = z.reshape(batch, seq, z.shape[-1])
    ba = ba.reshape(batch, seq, ba.shape[-1])
    pad = jnp.zeros((n_vh,), F32)
    gate_params = jnp.stack([jnp.concatenate([pad, a_log]), jnp.concatenate([pad, dt_bias])])
    blk = lambda n: pl.BlockSpec((None, tt, n), lambda b, t: (b, t, 0))
    out = pl.pallas_call(
        _gdn_kernel, grid=(batch, seq // tt),
        in_specs=[blk(qkv.shape[-1]), blk(z.shape[-1]), blk(ba.shape[-1]),
                  _resident(gate_params.shape), _resident((1, dk))],
        out_specs=blk(n_vh * dk),
        out_shape=jax.ShapeDtypeStruct((batch, seq, n_vh * dk), BF16),
        scratch_shapes=[pltpu.VMEM((n_vh, dk, dk), F32),
                        pltpu.VMEM((n_vh, tt, dk), F32),
                        pltpu.VMEM((n_vh, tt // c, 2 * c, dk), BF16),
                        pltpu.VMEM((n_vh, tt, dk), BF16),
                        pltpu.VMEM((n_vh, tt, c), BF16)],
        compiler_params=_params("parallel", "arbitrary"), name="gdn_core",
    )(qkv, z, ba, gate_params, norm_w.reshape(1, dk))
    return out.reshape(batch * seq, n_vh * dk)


def _hybrid_layer(x, cos, sin, npre, npost, w_in_all, i, conv_w, conv_b, gate_a_w, gate_a_b, gate_x_w, gate_x_b,
                  lam, q_norm, w_uq, kv_norm, w_ukv, w_out_all, batch, seq, x_is_ours):
    rg_w = lam.shape[0]
    d_model = w_in_all.shape[1]
    o3 = 2 * rg_w + Q_LORA
    assert (2 * rg_w) % Q_LORA == 0 and w_out_all.shape[1] == 2 * rg_w
    w_ckv = jnp.concatenate([w_in_all[i][:, o3:], jnp.zeros((d_model, LANES - MLA_ROPE), BF16)], axis=1)
    weights = [_Window(w_in_all, (None, d_model, rg_w), (i, 0, 0)),
               _Window(w_in_all, (None, d_model, rg_w), (i, 0, 1)),
               _Window(w_in_all, (None, d_model, Q_LORA), (i, 0, 2 * rg_w // Q_LORA)), w_ckv]

    wq = w_uq.reshape(Q_LORA, MLA_HEADS, MLA_QK)
    wq = jnp.concatenate([wq, jnp.zeros((Q_LORA, MLA_HEADS, ATTN_HEAD_WIDTH - MLA_QK), wq.dtype)], axis=2)
    wq = wq.reshape(Q_LORA, MLA_HEADS * ATTN_HEAD_WIDTH)
    wkv = w_ukv.reshape(KV_LORA, MLA_HEADS, MLA_NOPE + MLA_V)
    wk = wkv[:, :, :MLA_NOPE].reshape(KV_LORA, MLA_HEADS * MLA_NOPE)
    wv = wkv[:, :, MLA_NOPE:].reshape(KV_LORA, MLA_HEADS * MLA_V)
    x_r, gate_r, q, k, v = _hybrid_in_proj(x, npre, weights, cos, sin, q_norm, kv_norm, wq.astype(BF16),
                                           wk.astype(BF16), wv.astype(BF16))
    y_a = _rglru(x_r, gate_r, conv_w, conv_b, gate_a_w.astype(BF16), gate_a_b, gate_x_w.astype(BF16),
                 gate_x_b, lam, batch, seq)
    y_b = _attention(q, k, v, batch, seq)

    d_out = w_out_all.shape[2]
    w_out = [_Window(w_out_all, (None, rg_w, d_out), (i, 0, 0)), _Window(w_out_all, (None, rg_w, d_out), (i, 1, 0))]
    return _out_proj([y_a, y_b], w_out, x, npost, in_place=x_is_ours)


def _gdn_layer(x, npre, npost, w_in_all, i, conv_w, a_log, dt_bias, norm_w, w_out_all, batch, seq):
    d_model = w_in_all.shape[1]
    conv_c = conv_w.shape[1]
    v_w = GDN_V_HEADS * GDN_HEAD_DIM
    assert conv_c % v_w == 0
    qkv, z, ba = _gdn_in_proj(x, npre, _Window(w_in_all, (None, d_model, conv_c), (i, 0, 0)),
                              _Window(w_in_all, (None, d_model, v_w), (i, 0, conv_c // v_w)),
                              w_in_all[i][:, conv_c + v_w:], conv_w, seq)
    y = _gdn_core(qkv, z, ba, a_log, dt_bias, norm_w, batch, seq)
    return _out_proj([y], [_layer_window(w_out_all, i)], x, npost)


def kernel(x, positions, norm_mix_pre, norm_mix_post, norm_ffn_pre, norm_ffn_post,
           hy_w_in, rg_conv_w, rg_conv_b, rg_gate_a_w, rg_gate_a_b, rg_gate_x_w,
           rg_gate_x_b, rg_lambda, mla_q_norm, mla_w_uq, mla_kv_norm, mla_w_ukv, hy_w_out,
           gdn_w_in, gdn_conv_w, gdn_a_log, gdn_dt_bias, gdn_norm, gdn_w_out,
           ffn_w_gate, ffn_w_up, ffn_w_down):
    batch, seq, d = x.shape
    depth = norm_mix_pre.shape[0]
    cos, sin = _rope_tables(positions)
    wg_all, wu_all, wd_all, hy_in_all, hy_out_all, gdn_in_all, gdn_out_all = (
        w.astype(BF16) for w in (ffn_w_gate, ffn_w_up, ffn_w_down, hy_w_in, hy_w_out, gdn_w_in, gdn_w_out))
    x = x.reshape(batch * seq, d)
    for layer in range(depth):
        i = layer // 2
        if layer % 2 == 0:
            x = _hybrid_layer(x, cos, sin, norm_mix_pre[layer], norm_mix_post[layer], hy_in_all, i,
                              rg_conv_w[i], rg_conv_b[i], rg_gate_a_w[i], rg_gate_a_b[i], rg_gate_x_w[i],
                              rg_gate_x_b[i], rg_lambda[i], mla_q_norm[i], mla_w_uq[i], mla_kv_norm[i],
                              mla_w_ukv[i], hy_out_all, batch, seq, x_is_ours=layer > 0)
        else:
            x = _gdn_layer(x, norm_mix_pre[layer], norm_mix_post[layer], gdn_in_all, i, gdn_conv_w[i],
                           gdn_a_log[i], gdn_dt_bias[i], gdn_norm[i], gdn_out_all, batch, seq)
        x = _ffn(x, norm_ffn_pre[layer], wg_all, wu_all, wd_all, norm_ffn_post[layer], layer)
    return x.reshape(batch, seq, d)
```

```python
import functools
from typing import NamedTuple

import jax
import jax.numpy as jnp
from jax import lax
from jax.experimental import pallas as pl
from jax.experimental.pallas import tpu as pltpu

F32 = jnp.float32
BF16 = jnp.bfloat16

NORM_EPS = 1e-6
LANES = 128
SUBLANES = 8
VMEM_LIMIT_BYTES = 48 * 1024 * 1024

CONV_WIDTH = 4
RG_BLOCKS = 8
RG_C = 8.0
MLA_HEADS = 8
MLA_NOPE = 128
MLA_ROPE = 64
MLA_V = 128
MLA_QK = MLA_NOPE + MLA_ROPE
Q_LORA = 512
KV_LORA = 256
ROPE_THETA = 10000.0
GDN_K_HEADS = 8
GDN_V_HEADS = 16
GDN_HEAD_DIM = 128
GDN_CHUNK_ROWS = 128
GDN_SOLVE_BLOCK = 16
GDN_PROJ_SUBTILE = 128
GDN_SYSTEM_BATCH = 16

ROW_TILE = 1024
HYBRID_PROJ_ROW_TILE = 512
FFN_ROW_TILE = 512
FFN_HIDDEN_TILE = 256
ATTN_TILE = 256
ATTN_BLOCK_MULTIPLES = (4, 2, 1)
RG_TIME_TILE = 256
GDN_TIME_TILE = 256


def _params(*semantics):
    return pltpu.CompilerParams(dimension_semantics=semantics, vmem_limit_bytes=VMEM_LIMIT_BYTES)


def _rms(x, w):
    return x * lax.rsqrt(jnp.mean(x * x, axis=-1, keepdims=True) + NORM_EPS) * w


def _mm(a, b):
    return jnp.dot(a.astype(BF16), b.astype(BF16), preferred_element_type=F32)


def _mm_f32(a, b):
    return jnp.dot(a, b, preferred_element_type=F32, precision=lax.Precision.HIGHEST)


def _mm_nt(a, b):
    return lax.dot_general(a.astype(BF16), b.astype(BF16), (((1,), (1,)), ((), ())),
                           preferred_element_type=F32)


def _mm_tn(a, b):
    return lax.dot_general(a.astype(BF16), b.astype(BF16), (((0,), (0,)), ((), ())),
                           preferred_element_type=F32)


def _resident(shape):
    zeros = (0,) * len(shape)
    return pl.BlockSpec(shape, lambda *_: zeros, pipeline_mode=pl.Buffered(1))


class _Window(NamedTuple):
    array: jax.Array
    shape: tuple
    index: tuple


def _weight_spec(w):
    if isinstance(w, _Window):
        return pl.BlockSpec(w.shape, lambda *_: w.index, pipeline_mode=pl.Buffered(1))
    return _resident(w.shape)


def _weight_array(w):
    return w.array if isinstance(w, _Window) else w


def _weight_cols(w):
    return w.shape[-1]


def _shift_matrices(t):
    r = lax.broadcasted_iota(jnp.int32, (t, t), 0)
    c = lax.broadcasted_iota(jnp.int32, (t, t), 1)
    return [(r - c == s).astype(BF16) for s in range(1, CONV_WIDTH)]


def _causal_conv(x_bf16, prev_tail, w, shifts):
    x = x_bf16.astype(F32)
    y = x * w[CONV_WIDTH - 1:CONV_WIDTH]
    row = lax.broadcasted_iota(jnp.int32, prev_tail.shape, 0)
    head_fix = jnp.zeros(prev_tail.shape, F32)
    for s, shift in zip(range(1, CONV_WIDTH), shifts):
        w_s = w[CONV_WIDTH - 1 - s:CONV_WIDTH - s]
        y = y + jnp.dot(shift, x_bf16, preferred_element_type=F32) * w_s
        head_fix = head_fix + jnp.where(row < s, pltpu.roll(prev_tail, s, axis=0), 0.0) * w_s
    return jnp.concatenate([y[:SUBLANES] + head_fix, y[SUBLANES:]], axis=0), x


def _causal_conv_vpu(x, prev_tail, w):
    assert CONV_WIDTH == 4
    row = lax.broadcasted_iota(jnp.int32, prev_tail.shape, 0)

    def shift(v, v_tail, s):
        vs = pltpu.roll(v, s, axis=0)
        first = jnp.where(row < s, pltpu.roll(v_tail, s, axis=0), vs[:SUBLANES])
        return jnp.concatenate([first, vs[SUBLANES:]], axis=0)

    w0, w1, w2, w3 = (w[i:i + 1] for i in range(CONV_WIDTH))
    x1 = shift(x, prev_tail, 1)
    p = w1 * x + w0 * x1
    p_tail = w1 * prev_tail + w0 * pltpu.roll(prev_tail, 1, axis=0)
    return w3 * x + w2 * x1 + shift(p, p_tail, 2)


def _sigmoid(x):
    return 0.5 + 0.5 * jnp.tanh(0.5 * x)


def _silu(x):
    h = 0.5 * x
    return h + h * jnp.tanh(h)


def _softplus(x):
    return jnp.maximum(x, 0.0) + jnp.log1p(jnp.exp(-jnp.abs(x)))


def _ffn_kernel(*refs, n_in, hidden_tile):
    a_refs, w_refs = refs[:n_in], refs[n_in:2 * n_in]
    x_ref, nmix_ref, npre_ref, wg_ref, wu_ref, wd_ref, npost_ref, o_ref = refs[2 * n_in:]
    mix = jnp.dot(a_refs[0][...], w_refs[0][...], preferred_element_type=F32)
    for a_ref, w_ref in zip(a_refs[1:], w_refs[1:]):
        mix = mix + jnp.dot(a_ref[...], w_ref[...], preferred_element_type=F32)
    x = x_ref[...] + _rms(mix, nmix_ref[...])
    h = _rms(x, npre_ref[...]).astype(BF16)
    hidden = wg_ref.shape[1]
    acc = None
    for c0 in range(0, hidden, hidden_tile):
        g = jnp.dot(h, wg_ref[:, c0:c0 + hidden_tile], preferred_element_type=F32)
        u = jnp.dot(h, wu_ref[:, c0:c0 + hidden_tile], preferred_element_type=F32)
        a = (_silu(g) * u).astype(BF16)
        part = jnp.dot(a, wd_ref[c0:c0 + hidden_tile, :], preferred_element_type=F32)
        acc = part if acc is None else acc + part
    o_ref[...] = x + _rms(acc, npost_ref[...])


def _layer_window(stacked, layer):
    return _Window(stacked, (None,) + stacked.shape[1:], (layer, 0, 0))


def _mix_out_ffn(acts, out_weights, x, nmix, npre, wg_all, wu_all, wd_all, npost, layer, in_place):
    m, d = x.shape
    tm = min(FFN_ROW_TILE, m)
    n_in = len(acts)
    hidden = wg_all.shape[2]
    windows = [_layer_window(w, layer) for w in (wg_all, wu_all, wd_all)]
    hidden_tile = FFN_HIDDEN_TILE if hidden % FFN_HIDDEN_TILE == 0 else hidden
    row = lambda n: pl.BlockSpec((tm, n), lambda i: (i, 0))
    vec = _resident((1, d))
    return pl.pallas_call(
        functools.partial(_ffn_kernel, n_in=n_in, hidden_tile=hidden_tile),
        grid=(m // tm,),
        in_specs=[*[row(a.shape[1]) for a in acts], *[_weight_spec(w) for w in out_weights],
                  row(d), vec, vec, *[_weight_spec(w) for w in windows], vec],
        out_specs=row(d),
        out_shape=jax.ShapeDtypeStruct((m, d), F32),
        input_output_aliases={2 * n_in: 0} if in_place else {},
        compiler_params=_params("parallel"), name="mix_out_ffn",
    )(*acts, *[_weight_array(w) for w in out_weights], x, nmix.reshape(1, d), npre.reshape(1, d),
      wg_all, wu_all, wd_all, npost.reshape(1, d))


def _rope_table_kernel(pos_ref, invf_ref, cos_ref, sin_ref):
    ang = pos_ref[...].astype(F32) * invf_ref[...]
    cos_ref[...] = jnp.cos(ang)
    sin_ref[...] = jnp.sin(ang)


def _rope_tables(positions):
    m = positions.size
    tm = min(ROW_TILE, m)
    half = MLA_ROPE // 2
    inv_freq = 1.0 / (ROPE_THETA ** (jnp.arange(0, MLA_ROPE, 2, dtype=F32) / MLA_ROPE))
    invf = jnp.tile(inv_freq, LANES // half).reshape(1, LANES)
    return pl.pallas_call(
        _rope_table_kernel, grid=(m // tm,),
        in_specs=[pl.BlockSpec((tm, 1), lambda i: (i, 0)), _resident((1, LANES))],
        out_specs=[pl.BlockSpec((tm, LANES), lambda i: (i, 0))] * 2,
        out_shape=[jax.ShapeDtypeStruct((m, LANES), F32)] * 2,
        compiler_params=_params("parallel"), name="rope_tables",
    )(positions.reshape(m, 1), invf)


def _rglru_kernel(xr_ref, gr_ref, cw_ref, cb_ref, wa_ref, ba_ref, wx_ref, bx_ref, lam_ref, o_ref,
                  h_sc, tail_sc):
    @pl.when(pl.program_id(1) == 0)
    def _():
        h_sc[...] = jnp.zeros_like(h_sc)
        tail_sc[...] = jnp.zeros_like(tail_sc)

    t, width = xr_ref.shape
    xc, x = _causal_conv(xr_ref[...], tail_sc[...], cw_ref[...], _shift_matrices(t))
    xc = xc + cb_ref[...]
    tail_sc[...] = x[t - SUBLANES:]

    xcb = xc.astype(BF16)
    bw = width // RG_BLOCKS
    ra = jnp.concatenate([jnp.dot(xcb[:, n * bw:(n + 1) * bw], wa_ref[n], preferred_element_type=F32)
                          for n in range(RG_BLOCKS)], axis=1)
    rx = jnp.concatenate([jnp.dot(xcb[:, n * bw:(n + 1) * bw], wx_ref[n], preferred_element_type=F32)
                          for n in range(RG_BLOCKS)], axis=1)
    r = _sigmoid(ra + ba_ref[...])
    i = _sigmoid(rx + bx_ref[...])
    log_a = (-RG_C * _softplus(-lam_ref[...])) * r
    a = jnp.exp(log_a)
    th = jnp.tanh(log_a)
    b = jnp.sqrt(-2.0 * th / (1.0 - th)) * (i * xc)

    groups = t // SUBLANES
    a = a.reshape(groups, SUBLANES, width)
    b = b.reshape(groups, SUBLANES, width)
    sub = lax.broadcasted_iota(jnp.int32, (groups, SUBLANES, width), 1)
    s = 1
    while s < SUBLANES:
        keep = sub >= s
        b = jnp.where(keep, a * pltpu.roll(b, s, axis=1) + b, b)
        a = jnp.where(keep, a * pltpu.roll(a, s, axis=1), a)
        s *= 2
    carry = h_sc[0:1]
    hs = []
    for g in range(groups):
        hs.append(b[g] + a[g] * carry)
        carry = hs[-1][SUBLANES - 1:]
    h_sc[...] = jnp.broadcast_to(carry, h_sc.shape)
    h = jnp.concatenate(hs, axis=0)
    o_ref[...] = (h * jax.nn.gelu(gr_ref[...].astype(F32))).astype(o_ref.dtype)


def _rglru(x_r, gate_r, conv_w, conv_b, wa, ba, wx, bx, lam, batch, seq):
    width = x_r.shape[-1]
    tt = min(RG_TIME_TILE, seq)
    x_r = x_r.reshape(batch, seq, width)
    gate_r = gate_r.reshape(batch, seq, width)
    blk = pl.BlockSpec((None, tt, width), lambda b, t: (b, t, 0))
    vec = _resident((1, width))
    out = pl.pallas_call(
        _rglru_kernel, grid=(batch, seq // tt),
        in_specs=[blk, blk, _resident(conv_w.shape), vec, _resident(wa.shape), vec,
                  _resident(wx.shape), vec, vec],
        out_specs=blk,
        out_shape=jax.ShapeDtypeStruct((batch, seq, width), BF16),
        scratch_shapes=[pltpu.VMEM((SUBLANES, width), F32), pltpu.VMEM((SUBLANES, width), F32)],
        compiler_params=_params("parallel", "arbitrary"), name="rglru",
    )(x_r, gate_r, conv_w, conv_b.reshape(1, width), wa, ba.reshape(1, width), wx,
      bx.reshape(1, width), lam.reshape(1, width))
    return out.reshape(batch * seq, width)


ATTN_HEAD_WIDTH = 2 * LANES
LOG2_E = 1.4426950408889634


def _hybrid_in_proj_kernel(x_ref, npre_ref, wxr_ref, wgr_ref, wcq_ref, wckv_ref, cos_ref, sin_ref,
                           qnw_ref, kvnw_ref, wq_ref, wk_ref, wv_ref, xr_o, gr_o, q_o, k_o, v_o):
    tm = x_ref.shape[0]
    h = _rms(x_ref[...], npre_ref[...]).astype(BF16)
    xr_o[...] = jnp.dot(h, wxr_ref[...], preferred_element_type=F32).astype(xr_o.dtype)
    gr_o[...] = jnp.dot(h, wgr_ref[...], preferred_element_type=F32).astype(gr_o.dtype)
    c_q = jnp.dot(h, wcq_ref[...], preferred_element_type=F32)
    ckv = jnp.dot(h, wckv_ref[...], preferred_element_type=F32)
    hw = ATTN_HEAD_WIDTH
    half = MLA_ROPE // 2
    lane = lax.broadcasted_iota(jnp.int32, (tm, LANES), 1)
    cos = jnp.where(lane < MLA_ROPE, cos_ref[...], 0.0)
    sin = jnp.where(lane < MLA_ROPE, sin_ref[...], 0.0)

    def rope(x):
        rot = jnp.where(lane < half, -pltpu.roll(x, LANES - half, axis=1), pltpu.roll(x, half, axis=1))
        return x * cos + rot * sin

    q_scale = MLA_QK ** -0.5 * LOG2_E
    qn = _rms(c_q, qnw_ref[...]).astype(BF16)
    q = jnp.dot(qn, wq_ref[...], preferred_element_type=F32) * q_scale
    kvn = _rms(ckv[:, :KV_LORA], kvnw_ref[...]).astype(BF16)
    k_nope = jnp.dot(kvn, wk_ref[...], preferred_element_type=F32)
    v = jnp.dot(kvn, wv_ref[...], preferred_element_type=F32)
    k_rope = rope(ckv[:, KV_LORA:]).astype(k_o.dtype)
    ones = jnp.ones((tm, LANES), v_o.dtype)
    for h in range(MLA_HEADS):
        lo, mid, hi = h * hw, h * hw + LANES, (h + 1) * hw
        q_o[:, lo:mid] = q[:, lo:mid].astype(q_o.dtype)
        q_o[:, mid:hi] = rope(q[:, mid:hi]).astype(q_o.dtype)
        k_o[:, lo:mid] = k_nope[:, h * MLA_NOPE:(h + 1) * MLA_NOPE].astype(k_o.dtype)
        k_o[:, mid:hi] = k_rope
        v_o[:, lo:mid] = v[:, h * MLA_V:(h + 1) * MLA_V].astype(v_o.dtype)
        v_o[:, mid:hi] = ones


def _hybrid_in_proj(x, npre, in_weights, cos, sin, q_norm, kv_norm, wq, wk, wv):
    m, d = x.shape
    tm = min(HYBRID_PROJ_ROW_TILE, m)
    row = lambda n: pl.BlockSpec((tm, n), lambda i: (i, 0))
    rg_w = _weight_cols(in_weights[0])
    width = MLA_HEADS * ATTN_HEAD_WIDTH
    out_widths = [rg_w, rg_w, width, width, width]
    return pl.pallas_call(
        _hybrid_in_proj_kernel, grid=(m // tm,),
        in_specs=[row(d), _resident((1, d)), *[_weight_spec(w) for w in in_weights], row(LANES), row(LANES),
                  _resident((1, Q_LORA)), _resident((1, KV_LORA)), _resident(wq.shape),
                  _resident(wk.shape), _resident(wv.shape)],
        out_specs=[row(n) for n in out_widths],
        out_shape=[jax.ShapeDtypeStruct((m, n), BF16) for n in out_widths],
        compiler_params=_params("parallel"), name="hybrid_in_proj",
    )(x, npre.reshape(1, d), *[_weight_array(w) for w in in_weights], cos, sin,
      q_norm.reshape(1, Q_LORA), kv_norm.reshape(1, KV_LORA), wq, wk, wv)


def _attn_kernel(q_ref, k_ref, v_ref, o_ref, acc_sc, *, tile):
    qi = pl.program_id(1)
    hw = ATTN_HEAD_WIDTH
    heads = range(MLA_HEADS)
    row = lax.broadcasted_iota(jnp.int32, (tile, tile), 0)
    col = lax.broadcasted_iota(jnp.int32, (tile, tile), 1)
    acc_sc[...] = jnp.zeros_like(acc_sc)

    def block(start, width, m_prev, masked):
        rows = pl.ds(pl.multiple_of(start, tile), width)
        s, m_new, alpha, pv = [], [], [], []
        for h in heads:
            x = lax.dot_general(q_ref[:, h * hw:(h + 1) * hw], k_ref[rows, h * hw:(h + 1) * hw],
                                (((1,), (1,)), ((), ())), preferred_element_type=F32)
            if masked:
                x = jnp.where(row >= col, x, -jnp.inf)
            s.append(x)
            m_new.append(jnp.maximum(m_prev[h], jnp.max(x, axis=-1, keepdims=True)))
        for h in heads:
            p = jnp.exp2(s[h] - m_new[h]).astype(BF16)
            pv.append(jnp.dot(p, v_ref[rows, h * hw:(h + 1) * hw], preferred_element_type=F32))
            alpha.append(jnp.exp2(m_prev[h] - m_new[h]))
        for h in heads:
            acc_sc[h] = alpha[h] * acc_sc[h] + pv[h]
        return m_new

    m_run = [jnp.full((tile, 1), -jnp.inf, F32) for _ in heads]
    done = 0
    for mult in ATTN_BLOCK_MULTIPLES:
        count = (qi - done) // mult
        m_run = lax.fori_loop(
            0, count,
            lambda j, m, done=done, mult=mult: block((done + j * mult) * tile, mult * tile, m, False), m_run)
        done = done + count * mult
    block(qi * tile, tile, m_run, True)
    for h in heads:
        acc = acc_sc[h]
        o_ref[:, h * MLA_V:(h + 1) * MLA_V] = (acc[:, :MLA_V] / acc[:, LANES:LANES + MLA_V]).astype(o_ref.dtype)


def _attention(q, k, v, batch, seq):
    tile = min(ATTN_TILE, seq)
    width = q.shape[-1]
    r3 = lambda a: a.reshape(batch, seq, width)
    kv_blk = pl.BlockSpec((None, seq, width), lambda b, t: (b, 0, 0))
    out = pl.pallas_call(
        functools.partial(_attn_kernel, tile=tile),
        grid=(batch, seq // tile),
        in_specs=[pl.BlockSpec((None, tile, width), lambda b, t: (b, t, 0)), kv_blk, kv_blk],
        out_specs=pl.BlockSpec((None, tile, MLA_HEADS * MLA_V), lambda b, t: (b, t, 0)),
        out_shape=jax.ShapeDtypeStruct((batch, seq, MLA_HEADS * MLA_V), BF16),
        scratch_shapes=[pltpu.VMEM((MLA_HEADS, tile, ATTN_HEAD_WIDTH), F32)],
        compiler_params=_params("parallel", "arbitrary"), name="mla_attention",
    )(r3(q), r3(k), r3(v))
    return out.reshape(batch * seq, MLA_HEADS * MLA_V)


def _gdn_in_proj_kernel(x_ref, nw_ref, wqkv_ref, wz_ref, wba_ref, cw_ref, qkv_o, z_o, ba_o, tail_sc,
                        *, tiles_per_seq):
    @pl.when(pl.program_id(0) % tiles_per_seq == 0)
    def _():
        tail_sc[...] = jnp.zeros_like(tail_sc)

    dk = GDN_HEAD_DIM
    tm = x_ref.shape[0]
    h = _rms(x_ref[...], nw_ref[...]).astype(BF16)
    slab = 2 * dk
    n_qk = 2 * GDN_K_HEADS * dk
    n_slabs = wqkv_ref.shape[1] // slab
    z_every = n_slabs * slab // wz_ref.shape[1]
    for j in range(n_slabs):
        c0 = j * slab
        cols = slice(c0, c0 + slab)
        tail = tail_sc[:, cols]
        for r0 in range(0, tm, GDN_PROJ_SUBTILE):
            rows = slice(r0, r0 + GDN_PROJ_SUBTILE)
            y = jnp.dot(h[rows], wqkv_ref[:, cols], preferred_element_type=F32)
            act = _silu(_causal_conv_vpu(y, tail, cw_ref[:, cols]))
            tail = y[GDN_PROJ_SUBTILE - SUBLANES:]
            for c1 in range(0, slab, dk):
                a = act[:, c1:c1 + dk]
                if c0 < n_qk:
                    scale = dk ** -0.5 if c0 < n_qk // 2 else 1.0
                    a = a * (lax.rsqrt(jnp.sum(a * a, axis=-1, keepdims=True) + NORM_EPS) * scale)
                qkv_o[rows, c0 + c1:c0 + c1 + dk] = a.astype(qkv_o.dtype)
        tail_sc[:, cols] = tail
        if j % z_every == z_every - 1:
            zc = slice((j // z_every) * slab, (j // z_every + 1) * slab)
            z_o[:, zc] = jnp.dot(h, wz_ref[:, zc], preferred_element_type=F32).astype(z_o.dtype)
    ba_o[...] = jnp.dot(h, wba_ref[...], preferred_element_type=F32)


def _gdn_in_proj(x, nw, w_qkv, w_z, w_ba, conv_w, seq):
    m, d = x.shape
    tm = min(ROW_TILE, seq)
    row = lambda n: pl.BlockSpec((tm, n), lambda i: (i, 0))
    weights = [w_qkv, w_z, w_ba]
    widths = [_weight_cols(w) for w in weights]
    return pl.pallas_call(
        functools.partial(_gdn_in_proj_kernel, tiles_per_seq=seq // tm),
        grid=(m // tm,),
        in_specs=[row(d), _resident((1, d)), *[_weight_spec(w) for w in weights], _resident(conv_w.shape)],
        out_specs=[row(n) for n in widths],
        out_shape=[jax.ShapeDtypeStruct((m, n), dt) for n, dt in zip(widths, (BF16, BF16, F32))],
        scratch_shapes=[pltpu.VMEM((SUBLANES, widths[0]), F32)],
        compiler_params=_params("arbitrary"), name="gdn_in_proj",
    )(x, nw.reshape(1, d), *[_weight_array(w) for w in weights], conv_w)


def _nilpotent_inverse_batch(mats, order, eye):
    xs = [eye - m for m in mats]
    ps = mats
    k = 1
    while 2 * k < order:
        ps = [_mm(p, p) for p in ps]
        xs = [x + _mm(x, p) for x, p in zip(xs, ps)]
        k *= 2
    return xs


def _unit_lower_inverse_batch(mats):
    c = mats[0].shape[0]
    r = lax.broadcasted_iota(jnp.int32, (c, c), 0)
    q = lax.broadcasted_iota(jnp.int32, (c, c), 1)
    eye = (r == q).astype(F32)
    same_block = (r // GDN_SOLVE_BLOCK) == (q // GDN_SOLVE_BLOCK)
    ds = [jnp.where(same_block, a, 0.0) for a in mats]
    lows = [a - d for a, d in zip(mats, ds)]
    xs = _nilpotent_inverse_batch(ds, GDN_SOLVE_BLOCK, eye)
    ns = [_mm(x, low) for x, low in zip(xs, lows)]
    ys = _nilpotent_inverse_batch(ns, c // GDN_SOLVE_BLOCK, eye)
    return [_mm(y, x) for y, x in zip(ys, xs)]


def _gdn_kernel(qkv_ref, z_ref, ba_ref, gp_ref, nw_ref, o_ref, s_sc, u_sc, wq_sc, kd_sc, qk_sc):
    @pl.when(pl.program_id(1) == 0)
    def _():
        s_sc[...] = jnp.zeros_like(s_sc)

    dk = GDN_HEAD_DIM
    n_kh, n_vh = GDN_K_HEADS, GDN_V_HEADS
    rep = n_vh // n_kh
    c = GDN_CHUNK_ROWS
    t = qkv_ref.shape[0]
    n_chunks = t // c

    def head_cols(j):
        return slice(j * dk, (j + 1) * dk)

    ba = ba_ref[...]
    lane = lax.broadcasted_iota(jnp.int32, ba.shape, 1)
    gates = jnp.where(lane >= n_vh, -jnp.exp(gp_ref[0:1]) * _softplus(ba + gp_ref[1:2]),
                      _sigmoid(ba))

    ri = lax.broadcasted_iota(jnp.int32, (c, c), 0)
    ci = lax.broadcasted_iota(jnp.int32, (c, c), 1)
    incl = (ci <= ri).astype(F32)
    incl_t = (ri <= ci).astype(F32)
    lower = ri >= ci
    strict = ri > ci

    rows_of = [slice(n * c, (n + 1) * c) for n in range(n_chunks)]
    gch = [gates[r] for r in rows_of]
    gc = [_mm_f32(incl, g) for g in gch]
    gc_t = [lax.dot_general(g, incl_t, (((0,), (0,)), ((), ())), preferred_element_type=F32,
                            precision=lax.Precision.HIGHEST) for g in gch]
    chunk_decay = {(n, h): jnp.exp(gc[n][c - 1:c, n_vh + h:n_vh + h + 1])
                   for n in range(n_chunks) for h in range(n_vh)}

    systems = [(n, h) for n in range(n_chunks) for h in range(n_vh)]
    for s0 in range(0, len(systems), GDN_SYSTEM_BATCH):
        batch = systems[s0:s0 + GDN_SYSTEM_BATCH]
        pairs = sorted({(n, h // rep) for n, h in batch})
        gcol = {(n, h): gc[n][:, n_vh + h:n_vh + h + 1] for n, h in batch}
        beta = {(n, h): gch[n][:, h:h + 1] for n, h in batch}
        decay = {}
        for n, h in batch:
            diff = gcol[n, h] - gc_t[n][n_vh + h:n_vh + h + 1, :]
            decay[n, h] = jnp.where(lower, jnp.exp(jnp.where(lower, diff, 0.0)), 0.0)
        q16 = {(n, kh): qkv_ref[rows_of[n], head_cols(kh)] for n, kh in pairs}
        k16 = {(n, kh): qkv_ref[rows_of[n], head_cols(n_kh + kh)] for n, kh in pairs}
        k32 = {p: k16[p].astype(F32) for p in pairs}
        kb = {(n, h): k32[n, h // rep] * beta[n, h] for n, h in batch}
        kk = {(n, h): _mm_nt(kb[n, h], k16[n, h // rep]) for n, h in batch}
        qk = {p: _mm_nt(q16[p], k16[p]) for p in pairs}
        t_inv = _unit_lower_inverse_batch([jnp.where(strict, kk[s] * decay[s], 0.0) for s in batch])
        egc = {s: jnp.exp(gcol[s]) for s in batch}
        uw = [_mm(ti, jnp.concatenate(
            [qkv_ref[rows_of[n], head_cols(2 * n_kh + h)].astype(F32) * beta[n, h], kb[n, h] * egc[n, h]], axis=1))
            for ti, (n, h) in zip(t_inv, batch)]
        for (n, h), uw_s in zip(batch, uw):
            rows = rows_of[n]
            u_sc[h, rows] = uw_s[:, :dk]
            wq_sc[h, n, :c] = uw_s[:, dk:].astype(BF16)
            wq_sc[h, n, c:] = (q16[n, h // rep].astype(F32) * egc[n, h]).astype(BF16)
            gl = gc[n][c - 1:c, n_vh + h:n_vh + h + 1]
            kd_sc[h, rows] = (k32[n, h // rep] * jnp.exp(gl - gcol[n, h])).astype(BF16)
            qk_sc[h, rows] = (qk[n, h // rep] * decay[n, h]).astype(BF16)

    states = [s_sc[h] for h in range(n_vh)]
    for n in range(n_chunks):
        rows = slice(n * c, (n + 1) * c)
        ws_qs = [jnp.dot(wq_sc[h, n], states[h].astype(BF16), preferred_element_type=F32) for h in range(n_vh)]
        v_new = [(u_sc[h, rows] - ws_qs[h][:c]).astype(BF16) for h in range(n_vh)]
        outs = [ws_qs[h][c:] + jnp.dot(qk_sc[h, rows], v_new[h], preferred_element_type=F32)
                for h in range(n_vh)]
        states = [states[h] * chunk_decay[n, h]
                  + lax.dot_general(kd_sc[h, rows], v_new[h], (((0,), (0,)), ((), ())),
                                    preferred_element_type=F32) for h in range(n_vh)]
        for h in range(n_vh):
            cols = slice(h * dk, (h + 1) * dk)
            o = _rms(outs[h], nw_ref[...]) * _silu(z_ref[rows, cols].astype(F32))
            o_ref[rows, cols] = o.astype(o_ref.dtype)
    for h in range(n_vh):
        s_sc[h] = states[h]


def _gdn_core(qkv, z, ba, a_log, dt_bias, norm_w, batch, seq):
    dk = GDN_HEAD_DIM
    n_kh, n_vh = GDN_K_HEADS, GDN_V_HEADS
    tt = min(GDN_TIME_TILE, seq)
    c = GDN_CHUNK_ROWS
    qkv = qkv.reshape(batch, seq, qkv.shape[-1])
    z = z.reshape(batch, seq, z.shape[-1])
    ba = ba.reshape(batch, seq, ba.shape[-1])
    pad = jnp.zeros((n_vh,), F32)
    gate_params = jnp.stack([jnp.concatenate([pad, a_log]), jnp.concatenate([pad, dt_bias])])
    blk = lambda n: pl.BlockSpec((None, tt, n), lambda b, t: (b, t, 0))
    out = pl.pallas_call(
        _gdn_kernel, grid=(batch, seq // tt),
        in_specs=[blk(qkv.shape[-1]), blk(z.shape[-1]), blk(ba.shape[-1]),
                  _resident(gate_params.shape), _resident((1, dk))],
        out_specs=blk(n_vh * dk),
        out_shape=jax.ShapeDtypeStruct((batch, seq, n_vh * dk), BF16),
        scratch_shapes=[pltpu.VMEM((n_vh, dk, dk), F32),
                        pltpu.VMEM((n_vh, tt, dk), F32),
                        pltpu.VMEM((n_vh, tt // c, 2 * c, dk), BF16),
                        pltpu.VMEM((n_vh, tt, dk), BF16),
                        pltpu.VMEM((n_vh, tt, c), BF16)],
        compiler_params=_params("parallel", "arbitrary"), name="gdn_core",
    )(qkv, z, ba, gate_params, norm_w.reshape(1, dk))
    return out.reshape(batch * seq, n_vh * dk)


def _hybrid_layer(x, cos, sin, npre, w_in_all, i, conv_w, conv_b, gate_a_w, gate_a_b, gate_x_w, gate_x_b,
                  lam, q_norm, w_uq, kv_norm, w_ukv, w_out_all, batch, seq):
    rg_w = lam.shape[0]
    d_model = w_in_all.shape[1]
    o3 = 2 * rg_w + Q_LORA
    assert (2 * rg_w) % Q_LORA == 0 and w_out_all.shape[1] == 2 * rg_w
    w_ckv = jnp.concatenate([w_in_all[i][:, o3:], jnp.zeros((d_model, LANES - MLA_ROPE), BF16)], axis=1)
    weights = [_Window(w_in_all, (None, d_model, rg_w), (i, 0, 0)),
               _Window(w_in_all, (None, d_model, rg_w), (i, 0, 1)),
               _Window(w_in_all, (None, d_model, Q_LORA), (i, 0, 2 * rg_w // Q_LORA)), w_ckv]

    wq = w_uq.reshape(Q_LORA, MLA_HEADS, MLA_QK)
    wq = jnp.concatenate([wq, jnp.zeros((Q_LORA, MLA_HEADS, ATTN_HEAD_WIDTH - MLA_QK), wq.dtype)], axis=2)
    wq = wq.reshape(Q_LORA, MLA_HEADS * ATTN_HEAD_WIDTH)
    wkv = w_ukv.reshape(KV_LORA, MLA_HEADS, MLA_NOPE + MLA_V)
    wk = wkv[:, :, :MLA_NOPE].reshape(KV_LORA, MLA_HEADS * MLA_NOPE)
    wv = wkv[:, :, MLA_NOPE:].reshape(KV_LORA, MLA_HEADS * MLA_V)
    x_r, gate_r, q, k, v = _hybrid_in_proj(x, npre, weights, cos, sin, q_norm, kv_norm, wq.astype(BF16),
                                           wk.astype(BF16), wv.astype(BF16))
    y_a = _rglru(x_r, gate_r, conv_w, conv_b, gate_a_w.astype(BF16), gate_a_b, gate_x_w.astype(BF16),
                 gate_x_b, lam, batch, seq)
    y_b = _attention(q, k, v, batch, seq)

    d_out = w_out_all.shape[2]
    w_out = [_Window(w_out_all, (None, rg_w, d_out), (i, 0, 0)), _Window(w_out_all, (None, rg_w, d_out), (i, 1, 0))]
    return [y_a, y_b], w_out


def _gdn_layer(x, npre, w_in_all, i, conv_w, a_log, dt_bias, norm_w, w_out_all, batch, seq):
    d_model = w_in_all.shape[1]
    conv_c = conv_w.shape[1]
    v_w = GDN_V_HEADS * GDN_HEAD_DIM
    assert conv_c % v_w == 0
    qkv, z, ba = _gdn_in_proj(x, npre, _Window(w_in_all, (None, d_model, conv_c), (i, 0, 0)),
                              _Window(w_in_all, (None, d_model, v_w), (i, 0, conv_c // v_w)),
                              w_in_all[i][:, conv_c + v_w:], conv_w, seq)
    y = _gdn_core(qkv, z, ba, a_log, dt_bias, norm_w, batch, seq)
    return [y], [_layer_window(w_out_all, i)]


def kernel(x, positions, norm_mix_pre, norm_mix_post, norm_ffn_pre, norm_ffn_post,
           hy_w_in, rg_conv_w, rg_conv_b, rg_gate_a_w, rg_gate_a_b, rg_gate_x_w,
           rg_gate_x_b, rg_lambda, mla_q_norm, mla_w_uq, mla_kv_norm, mla_w_ukv, hy_w_out,
           gdn_w_in, gdn_conv_w, gdn_a_log, gdn_dt_bias, gdn_norm, gdn_w_out,
           ffn_w_gate, ffn_w_up, ffn_w_down):
    batch, seq, d = x.shape
    depth = norm_mix_pre.shape[0]
    cos, sin = _rope_tables(positions)
    wg_all, wu_all, wd_all, hy_in_all, hy_out_all, gdn_in_all, gdn_out_all = (
        w.astype(BF16) for w in (ffn_w_gate, ffn_w_up, ffn_w_down, hy_w_in, hy_w_out, gdn_w_in, gdn_w_out))
    x = x.reshape(batch * seq, d)
    for layer in range(depth):
        i = layer // 2
        if layer % 2 == 0:
            acts, w_out = _hybrid_layer(x, cos, sin, norm_mix_pre[layer], hy_in_all, i,
                                        rg_conv_w[i], rg_conv_b[i], rg_gate_a_w[i], rg_gate_a_b[i], rg_gate_x_w[i],
                                        rg_gate_x_b[i], rg_lambda[i], mla_q_norm[i], mla_w_uq[i], mla_kv_norm[i],
                                        mla_w_ukv[i], hy_out_all, batch, seq)
        else:
            acts, w_out = _gdn_layer(x, norm_mix_pre[layer], gdn_in_all, i, gdn_conv_w[i],
                                     gdn_a_log[i], gdn_dt_bias[i], gdn_norm[i], gdn_out_all, batch, seq)
        x = _mix_out_ffn(acts, w_out, x, norm_mix_post[layer], norm_ffn_pre[layer], wg_all, wu_all, wd_all,
                         norm_ffn_post[layer], layer, in_place=layer > 0)
    return x.reshape(batch, seq, d)
```
